```python
import jax, jax.numpy as jnp
from jax import lax
import numpy as np

D_MODEL = 4096
BATCH = 16
SEQ = 256
DEPTH = 4
DEC_BATCH = 8
DEC_SEQ = 2048
PAST_LEN = 512

GRID_W = 64
N_MIXERS = 3
N_MLA = (DEPTH + 2) // 3
N_GQA = (DEPTH + 1) // 3
N_SWA = DEPTH // 3

MLA_HEADS = D_MODEL // 128
MLA_Q_LORA = 1024
MLA_KV_LORA = 512
MLA_NOPE = 128
MLA_ROPE = 64
MLA_V = 128
MLA_SCALE = (MLA_NOPE + MLA_ROPE) ** -0.5

GQA_HEADS = D_MODEL // 128
GQA_KV_HEADS = 8
GQA_HEAD_DIM = 128
GQA_SCALE = GQA_HEAD_DIM ** -0.5

SWA_HEADS = D_MODEL // 64
SWA_KV_HEADS = 8
SWA_HEAD_DIM = 64
SWA_SCALE = SWA_HEAD_DIM ** -0.5
WINDOW = 128

D_FF = ((8 * D_MODEL // 3) + 255) // 256 * 256
CONV_W = 3
Q_BLOCK = 128
ROPE_BASE = 10000.0
EPS = 1e-6
NEG_INF = -1e30

kernel_name = "hybrid_dit_mla_gqa2d_swasink_convffn_step"


def rmsnorm(x, g):
    xf = x.astype(jnp.float32)
    y = xf * lax.rsqrt(jnp.mean(xf * xf, axis=-1, keepdims=True) + EPS)
    return (y * g.astype(jnp.float32)).astype(x.dtype)


def adaln_mod(cvec, w, b):
    m = jax.nn.silu(cvec) @ w + b
    return m.reshape(cvec.shape[0], 6, 1, D_MODEL)


def modulate(h, shift, scale):
    return h * (1 + scale) + shift


def _rope_axis(x, pos):
    half = x.shape[-1] // 2
    freqs = ROPE_BASE ** (-jnp.arange(half, dtype=jnp.float32) / half)
    ang = pos.astype(jnp.float32)[:, None] * freqs
    cos = jnp.cos(ang)[:, None, :]
    sin = jnp.sin(ang)[:, None, :]
    x1 = x[..., :half].astype(jnp.float32)
    x2 = x[..., half:].astype(jnp.float32)
    return jnp.concatenate([x1 * cos - x2 * sin, x1 * sin + x2 * cos], axis=-1).astype(x.dtype)


def rope_2d(x):
    n = x.shape[1]
    n_rows = n // GRID_W
    rows = jnp.repeat(jnp.arange(n_rows), GRID_W)
    cols = jnp.tile(jnp.arange(GRID_W), n_rows)
    d2 = x.shape[-1] // 2
    return jnp.concatenate([_rope_axis(x[..., :d2], rows), _rope_axis(x[..., d2:], cols)], axis=-1)


def _softmax_av(s, v, sink):
    m = jnp.max(s, axis=-1, keepdims=True)
    if sink is not None:
        m = jnp.maximum(m, sink)
    p = jnp.exp(s - m)
    den = jnp.sum(p, axis=-1, keepdims=True)
    if sink is not None:
        den = den + jnp.exp(sink - m)
    return jnp.einsum('bkgqm,bmkd->bqkgd', (p / den).astype(v.dtype), v)


def _query_blocks(q, kvh):
    b, n, h, d = q.shape
    nb = n // Q_BLOCK
    return q.reshape(b, nb, Q_BLOCK, kvh, h // kvh, d).transpose(1, 0, 2, 3, 4, 5)


def _merge_blocks(o):
    nb, b, qb, kvh, g, dv = o.shape
    return o.transpose(1, 0, 2, 3, 4, 5).reshape(b, nb * qb, kvh * g * dv)


def _sink_logits(sink, kvh):
    return sink.astype(jnp.float32).reshape(1, kvh, -1, 1, 1)


def dense_attention(q, k, v, scale, sink=None):
    kvh = k.shape[2]

    def one_block(qi):
        s = jnp.einsum('bqkgd,bmkd->bkgqm', qi, k, preferred_element_type=jnp.float32) * scale
        return _softmax_av(s, v, sink)

    return _merge_blocks(lax.map(one_block, _query_blocks(q, kvh)))


def windowed_attention(q, k, v, k_ctx, v_ctx, scale, sink):
    n = q.shape[1]
    kvh = k.shape[2]
    nb = n // Q_BLOCK
    span = 3 * Q_BLOCK
    pad = ((0, 0), (Q_BLOCK, Q_BLOCK), (0, 0), (0, 0))
    kp = jnp.pad(k, pad)
    vp = jnp.pad(v, pad)
    offs = jnp.arange(span) - Q_BLOCK
    qoff = jnp.arange(Q_BLOCK)

    def one_block(args):
        qi, blk = args
        start = blk * Q_BLOCK
        kb = lax.dynamic_slice_in_dim(kp, start, span, axis=1)
        vb = lax.dynamic_slice_in_dim(vp, start, span, axis=1)
        qpos = start + qoff
        kpos = start + offs
        valid = (jnp.abs(qpos[:, None] - kpos[None, :]) <= WINDOW) & (kpos >= 0)[None, :] & (kpos < n)[None, :]
        s_loc = jnp.einsum('bqkgd,bmkd->bkgqm', qi, kb, preferred_element_type=jnp.float32) * scale
        s_loc = jnp.where(valid, s_loc, NEG_INF)
        s_ctx = jnp.einsum('bqkgd,bmkd->bkgqm', qi, k_ctx, preferred_element_type=jnp.float32) * scale
        s = jnp.concatenate([s_ctx, s_loc], axis=-1)
        return _softmax_av(s, jnp.concatenate([v_ctx, vb], axis=1), sink)

    return _merge_blocks(lax.map(one_block, (_query_blocks(q, kvh), jnp.arange(nb))))


def mla_project(h, w_in, q_norm_g, kv_norm_g, w_uq):
    b, n, _ = h.shape
    cq, ckv, kpe = jnp.split(h @ w_in, [MLA_Q_LORA, MLA_Q_LORA + MLA_KV_LORA], axis=-1)
    q = (rmsnorm(cq, q_norm_g) @ w_uq).reshape(b, n, MLA_HEADS, MLA_NOPE + MLA_ROPE)
    return q, rmsnorm(ckv, kv_norm_g), kpe


def mla_expand(ckv, kpe, w_ukv):
    b, m, _ = ckv.shape
    kv = (ckv @ w_ukv).reshape(b, m, MLA_HEADS, MLA_NOPE + MLA_V)
    k_nope, v = jnp.split(kv, [MLA_NOPE], axis=-1)
    k_pe = jnp.broadcast_to(kpe[:, :, None, :], (b, m, MLA_HEADS, MLA_ROPE))
    return jnp.concatenate([k_nope, k_pe], axis=-1), v


def mla_context(h, w_in, q_norm_g, kv_norm_g, w_uq, w_ukv, w_o):
    q, ckv, kpe = mla_project(h, w_in, q_norm_g, kv_norm_g, w_uq)
    k, v = mla_expand(ckv, kpe, w_ukv)
    return dense_attention(q, k, v, MLA_SCALE) @ w_o, ckv, kpe


def mla_latent(h, ckv_ctx, kpe_ctx, w_in, q_norm_g, kv_norm_g, w_uq, w_ukv, w_o):
    q, ckv, kpe = mla_project(h, w_in, q_norm_g, kv_norm_g, w_uq)
    q = jnp.concatenate([q[..., :MLA_NOPE], rope_2d(q[..., MLA_NOPE:])], axis=-1)
    kpe = rope_2d(kpe[:, :, None, :])[:, :, 0]
    k_c, v_c = mla_expand(ckv_ctx, kpe_ctx, w_ukv)
    k_l, v_l = mla_expand(ckv, kpe, w_ukv)
    k = jnp.concatenate([k_c, k_l], axis=1)
    v = jnp.concatenate([v_c, v_l], axis=1)
    return dense_attention(q, k, v, MLA_SCALE) @ w_o


def split_qkv(h, w_qkv, n_heads, n_kv, hd):
    b, n, _ = h.shape
    q, k, v = jnp.split(h @ w_qkv, [n_heads * hd, (n_heads + n_kv) * hd], axis=-1)
    return q.reshape(b, n, n_heads, hd), k.reshape(b, n, n_kv, hd), v.reshape(b, n, n_kv, hd)


def gqa_context(h, w_qkv, q_g, k_g, w_o):
    q, k, v = split_qkv(h, w_qkv, GQA_HEADS, GQA_KV_HEADS, GQA_HEAD_DIM)
    q = rmsnorm(q, q_g)
    k = rmsnorm(k, k_g)
    return dense_attention(q, k, v, GQA_SCALE) @ w_o, k, v


def gqa_latent(h, k_ctx, v_ctx, w_qkv, q_g, k_g, w_o):
    q, k, v = split_qkv(h, w_qkv, GQA_HEADS, GQA_KV_HEADS, GQA_HEAD_DIM)
    q = rope_2d(rmsnorm(q, q_g))
    k = rope_2d(rmsnorm(k, k_g))
    k = jnp.concatenate([k_ctx, k], axis=1)
    v = jnp.concatenate([v_ctx, v], axis=1)
    return dense_attention(q, k, v, GQA_SCALE) @ w_o


def swa_context(h, w_qkv, sink, w_o):
    q, k, v = split_qkv(h, w_qkv, SWA_HEADS, SWA_KV_HEADS, SWA_HEAD_DIM)
    o = dense_attention(q, k, v, SWA_SCALE, _sink_logits(sink, SWA_KV_HEADS))
    return o @ w_o, k, v


def swa_latent(h, k_ctx, v_ctx, w_qkv, sink, w_o):
    q, k, v = split_qkv(h, w_qkv, SWA_HEADS, SWA_KV_HEADS, SWA_HEAD_DIM)
    q = rope_2d(q)
    k = rope_2d(k)
    o = windowed_attention(q, k, v, k_ctx, v_ctx, SWA_SCALE, _sink_logits(sink, SWA_KV_HEADS))
    return o @ w_o


def conv_ffn(h, w_in, conv_w, conv_b, w_out):
    u = h @ w_in
    n = u.shape[1]
    r = CONV_W // 2
    up = jnp.pad(u, ((0, 0), (r, r), (0, 0)))
    u = sum(up[:, t:t + n] * conv_w[t] for t in range(CONV_W)) + conv_b
    gate, val = jnp.split(u, 2, axis=-1)
    return (jax.nn.silu(gate) * val) @ w_out


def attn_residual(x, a, mod, g):
    return x + mod[:, 2] * rmsnorm(a, g[1])


def ffn_residual(x, mod, g, w_in, conv_w, conv_b, w_out):
    h = modulate(rmsnorm(x, g[2]), mod[:, 3], mod[:, 4])
    return x + mod[:, 5] * rmsnorm(conv_ffn(h, w_in, conv_w, conv_b, w_out), g[3])


def setup_inputs(seed: int = 0) -> dict:
    key = jax.random.key(seed)
    ks = iter(jax.random.split(key, 32))

    def nrm(shape, scale=1.0):
        return jax.random.normal(next(ks), shape, jnp.float32) * scale

    def w(shape):
        return nrm(shape, shape[-2] ** -0.5)

    def gain(shape):
        return 1.0 + nrm(shape, 0.05)

    L = PAST_LEN
    return {
        "x_prompt": nrm((BATCH, SEQ, D_MODEL)),
        "x_sample": nrm((DEC_BATCH, DEC_SEQ, D_MODEL)),
        "cache_mla_ckv": nrm((DEC_BATCH, N_MLA, L, MLA_KV_LORA)),
        "cache_mla_kpe": nrm((DEC_BATCH, N_MLA, L, MLA_ROPE)),
        "cache_gqa_k": nrm((DEC_BATCH, N_GQA, L, GQA_KV_HEADS, GQA_HEAD_DIM)),
        "cache_gqa_v": nrm((DEC_BATCH, N_GQA, L, GQA_KV_HEADS, GQA_HEAD_DIM)),
        "cache_swa_k": nrm((DEC_BATCH, N_SWA, L, SWA_KV_HEADS, SWA_HEAD_DIM)),
        "cache_swa_v": nrm((DEC_BATCH, N_SWA, L, SWA_KV_HEADS, SWA_HEAD_DIM)),
        "c": nrm((DEC_BATCH, D_MODEL)),
        "c_ctx": nrm((D_MODEL,)),
        "mod_w": w((DEPTH, D_MODEL, 6 * D_MODEL)),
        "mod_b": nrm((DEPTH, 6 * D_MODEL), 0.01),
        "norm_g": gain((DEPTH, 4, D_MODEL)),
        "ffn_w_in": w((DEPTH, D_MODEL, 2 * D_FF)),
        "ffn_conv_w": w((DEPTH, CONV_W, 2 * D_FF)),
        "ffn_conv_b": nrm((DEPTH, 2 * D_FF), 0.01),
        "ffn_w_out": w((DEPTH, D_FF, D_MODEL)),
        "mla_w_in": w((N_MLA, D_MODEL, MLA_Q_LORA + MLA_KV_LORA + MLA_ROPE)),
        "mla_q_norm_g": gain((N_MLA, MLA_Q_LORA)),
        "mla_kv_norm_g": gain((N_MLA, MLA_KV_LORA)),
        "mla_w_uq": w((N_MLA, MLA_Q_LORA, MLA_HEADS * (MLA_NOPE + MLA_ROPE))),
        "mla_w_ukv": w((N_MLA, MLA_KV_LORA, MLA_HEADS * (MLA_NOPE + MLA_V))),
        "mla_w_o": w((N_MLA, MLA_HEADS * MLA_V, D_MODEL)),
        "gqa_w_qkv": w((N_GQA, D_MODEL, (GQA_HEADS + 2 * GQA_KV_HEADS) * GQA_HEAD_DIM)),
        "gqa_q_norm_g": gain((N_GQA, GQA_HEAD_DIM)),
        "gqa_k_norm_g": gain((N_GQA, GQA_HEAD_DIM)),
        "gqa_w_o": w((N_GQA, GQA_HEADS * GQA_HEAD_DIM, D_MODEL)),
        "swa_w_qkv": w((N_SWA, D_MODEL, (SWA_HEADS + 2 * SWA_KV_HEADS) * SWA_HEAD_DIM)),
        "swa_sink": nrm((N_SWA, SWA_HEADS), 0.5),
        "swa_w_o": w((N_SWA, SWA_HEADS * SWA_HEAD_DIM, D_MODEL)),
    }


def reference(x_prompt, x_sample, cache_mla_ckv, cache_mla_kpe, cache_gqa_k, cache_gqa_v,
              cache_swa_k, cache_swa_v, c, c_ctx, mod_w, mod_b, norm_g, ffn_w_in, ffn_conv_w,
              ffn_conv_b, ffn_w_out, mla_w_in, mla_q_norm_g, mla_kv_norm_g, mla_w_uq, mla_w_ukv,
              mla_w_o, gqa_w_qkv, gqa_q_norm_g, gqa_k_norm_g, gqa_w_o, swa_w_qkv, swa_sink, swa_w_o):
    x = x_prompt
    ckv_l, kpe_l, gk_l, gv_l, sk_l, sv_l = [], [], [], [], [], []
    for i in range(DEPTH):
        kind, j = i % N_MIXERS, i // N_MIXERS
        mod = adaln_mod(c_ctx[None], mod_w[i], mod_b[i])
        h = modulate(rmsnorm(x, norm_g[i, 0]), mod[:, 0], mod[:, 1])
        if kind == 0:
            a, ckv, kpe = mla_context(h, mla_w_in[j], mla_q_norm_g[j], mla_kv_norm_g[j],
                                      mla_w_uq[j], mla_w_ukv[j], mla_w_o[j])
            ckv_l.append(ckv)
            kpe_l.append(kpe)
        elif kind == 1:
            a, k, v = gqa_context(h, gqa_w_qkv[j], gqa_q_norm_g[j], gqa_k_norm_g[j], gqa_w_o[j])
            gk_l.append(k)
            gv_l.append(v)
        else:
            a, k, v = swa_context(h, swa_w_qkv[j], swa_sink[j], swa_w_o[j])
            sk_l.append(k)
            sv_l.append(v)
        x = attn_residual(x, a, mod, norm_g[i])
        x = ffn_residual(x, mod, norm_g[i], ffn_w_in[i], ffn_conv_w[i], ffn_conv_b[i], ffn_w_out[i])
    y_prompt = x

    x = x_sample
    for i in range(DEPTH):
        kind, j = i % N_MIXERS, i // N_MIXERS
        mod = adaln_mod(c, mod_w[i], mod_b[i])
        h = modulate(rmsnorm(x, norm_g[i, 0]), mod[:, 0], mod[:, 1])
        if kind == 0:
            a = mla_latent(h, cache_mla_ckv[:, j], cache_mla_kpe[:, j], mla_w_in[j], mla_q_norm_g[j],
                           mla_kv_norm_g[j], mla_w_uq[j], mla_w_ukv[j], mla_w_o[j])
        elif kind == 1:
            a = gqa_latent(h, cache_gqa_k[:, j], cache_gqa_v[:, j], gqa_w_qkv[j], gqa_q_norm_g[j],
                           gqa_k_norm_g[j], gqa_w_o[j])
        else:
            a = swa_latent(h, cache_swa_k[:, j], cache_swa_v[:, j], swa_w_qkv[j], swa_sink[j], swa_w_o[j])
        x = attn_residual(x, a, mod, norm_g[i])
        x = ffn_residual(x, mod, norm_g[i], ffn_w_in[i], ffn_conv_w[i], ffn_conv_b[i], ffn_w_out[i])
    y_sample = x

    new_mla_ckv = jnp.stack(ckv_l, axis=1)
    new_mla_kpe = jnp.stack(kpe_l, axis=1)
    new_gqa_k = jnp.stack(gk_l, axis=1)
    new_gqa_v = jnp.stack(gv_l, axis=1)
    new_swa_k = jnp.stack(sk_l, axis=1)
    new_swa_v = jnp.stack(sv_l, axis=1)
    return (y_prompt, y_sample, new_mla_ckv, new_mla_kpe, new_gqa_k, new_gqa_v, new_swa_k, new_swa_v)
```

```python
import functools

import jax
import jax.numpy as jnp
from jax import lax
from jax.experimental import pallas as pl
from jax.experimental.pallas import tpu as pltpu

F32 = jnp.float32
BF16 = jnp.bfloat16

EPS = 1e-6
NEG_INF = -1e30
ROPE_BASE = 10000.0
GRID_W = 64
WINDOW = 128
LANE = 128
MOD_ROWS = 16
VMEM_LIMIT = 56 * 1024 * 1024


def _cp(*sem):
    return pltpu.CompilerParams(dimension_semantics=sem, vmem_limit_bytes=VMEM_LIMIT)


def _rms(x, g):
    ms = jnp.mean(x * x, axis=-1, keepdims=True)
    return x * lax.rsqrt(ms + EPS) * g


def _adaln_kernel(c_ref, w_ref, b_ref, o_ref):
    c = c_ref[...]
    s = c / (1.0 + jnp.exp(-c))
    o_ref[...] = jnp.dot(s.astype(BF16), w_ref[...].astype(BF16),
                         preferred_element_type=F32) + b_ref[...]


def adaln(cvec, mod_w, mod_b):
    n_layers, d, n = mod_w.shape
    tn = 512 if n % 512 == 0 else n
    return pl.pallas_call(
        _adaln_kernel,
        grid=(n_layers, n // tn),
        in_specs=[pl.BlockSpec((MOD_ROWS, d), lambda l, j: (0, 0)),
                  pl.BlockSpec((None, d, tn), lambda l, j: (l, 0, j)),
                  pl.BlockSpec((None, 1, tn), lambda l, j: (l, 0, j))],
        out_specs=pl.BlockSpec((None, MOD_ROWS, tn), lambda l, j: (l, 0, j)),
        out_shape=jax.ShapeDtypeStruct((n_layers, MOD_ROWS, n), F32),
        compiler_params=_cp("parallel", "parallel"),
        name="adaln",
    )(cvec, mod_w, mod_b.reshape(n_layers, 1, n))


class Rows:
    def __init__(self, n_ctx_rows, ctx_seq, n_lat_rows, lat_seq):
        self.tc, self.ctx_seq, self.tl, self.lat_seq = n_ctx_rows, ctx_seq, n_lat_rows, lat_seq
        self.t = n_ctx_rows + n_lat_rows

    def mod_row(self, i, tm):
        nctx = self.tc // tm
        per = self.lat_seq // tm
        return jnp.where(i < nctx, 0, 1 + (i - nctx) // per)


def _mod_spec(rows, tm, base, d):
    return pl.BlockSpec((None, 1, d), lambda i: (base + rows.mod_row(i, tm) * 6, 0, 0))


def _normmod_kernel(x_ref, g_ref, sh_ref, sc_ref, o_ref):
    y = _rms(x_ref[...], g_ref[...])
    o_ref[...] = (y * (1.0 + sc_ref[...]) + sh_ref[...]).astype(o_ref.dtype)


def normmod(x, g, mods, layer, comp, rows, tm):
    t, d = x.shape
    base = layer * MOD_ROWS * 6 + comp
    return pl.pallas_call(
        _normmod_kernel,
        grid=(t // tm,),
        in_specs=[pl.BlockSpec((tm, d), lambda i: (i, 0)),
                  pl.BlockSpec((1, d), lambda i: (0, 0)),
                  _mod_spec(rows, tm, base, d),
                  _mod_spec(rows, tm, base + 1, d)],
        out_specs=pl.BlockSpec((tm, d), lambda i: (i, 0)),
        out_shape=jax.ShapeDtypeStruct((t, d), BF16),
        compiler_params=_cp("parallel"),
        name="normmod",
    )(x, g.reshape(1, d), mods, mods)


def _resid_kernel(x_ref, a_ref, g_ref, gate_ref, o_ref):
    o_ref[...] = x_ref[...] + gate_ref[...] * _rms(a_ref[...], g_ref[...])


def resid(x, a, g, mods, layer, comp, rows, tm):
    t, d = x.shape
    base = layer * MOD_ROWS * 6 + comp
    return pl.pallas_call(
        _resid_kernel,
        grid=(t // tm,),
        in_specs=[pl.BlockSpec((tm, d), lambda i: (i, 0)),
                  pl.BlockSpec((tm, d), lambda i: (i, 0)),
                  pl.BlockSpec((1, d), lambda i: (0, 0)),
                  _mod_spec(rows, tm, base, d)],
        out_specs=pl.BlockSpec((tm, d), lambda i: (i, 0)),
        out_shape=jax.ShapeDtypeStruct((t, d), F32),
        compiler_params=_cp("parallel"),
        name="resid",
    )(x, a, g.reshape(1, d), mods)


def _mm_kernel(x_ref, w_ref, o_ref):
    o_ref[...] = jnp.dot(x_ref[...], w_ref[...], preferred_element_type=F32).astype(o_ref.dtype)


def matmul(x, w, out_dtype, tm, tn, single_x=False, single_w=False):
    m, k = x.shape
    n = w.shape[1]
    tm, tn = min(tm, m), min(tn, n)
    assert m % tm == 0 and n % tn == 0, (m, n, tm, tn)
    x_mode = dict(pipeline_mode=pl.Buffered(1)) if single_x else {}
    w_mode = dict(pipeline_mode=pl.Buffered(1)) if single_w else {}
    return pl.pallas_call(
        _mm_kernel,
        grid=(m // tm, n // tn),
        in_specs=[pl.BlockSpec((tm, k), lambda i, j: (i, 0), **x_mode),
                  pl.BlockSpec((k, tn), lambda i, j: (0, j), **w_mode)],
        out_specs=pl.BlockSpec((tm, tn), lambda i, j: (i, j)),
        out_shape=jax.ShapeDtypeStruct((m, n), out_dtype),
        compiler_params=_cp("parallel", "parallel"),
        name="matmul",
    )(x, w)


def _ffn_in_kernel(h_ref, wg_ref, wv_ref, cwg_ref, cwv_ref, cbg_ref, cbv_ref, o_ref, *,
                   n_ctx_blocks, ctx_seq, lat_seq):
    tm, tn = o_ref.shape
    seq = jnp.where(pl.program_id(0) < n_ctx_blocks, ctx_seq, lat_seq)
    pos = lax.broadcasted_iota(jnp.int32, (tm, tn), 0) & (seq - 1)
    first = pos == 0
    last = pos == seq - 1
    h = h_ref[...]

    def conv(w_ref, cw_ref, cb_ref):
        u = jnp.dot(h, w_ref[...], preferred_element_type=F32)
        prev = jnp.where(first, 0.0, pltpu.roll(u, 1, axis=0))
        nxt = jnp.where(last, 0.0, pltpu.roll(u, tm - 1, axis=0))
        return prev * cw_ref[0:1, :] + u * cw_ref[1:2, :] + nxt * cw_ref[2:3, :] + cb_ref[...]

    gate = conv(wg_ref, cwg_ref, cbg_ref)
    val = conv(wv_ref, cwv_ref, cbv_ref)
    o_ref[...] = (gate / (1.0 + jnp.exp(-gate)) * val).astype(o_ref.dtype)


def ffn_in(h, w_in, conv_w, conv_b, rows, tm, tn):
    t, d = h.shape
    f = w_in.shape[1] // 2
    assert f % tn == 0 and rows.tc % tm == 0 and tm % rows.ctx_seq == 0 and tm == rows.lat_seq
    assert rows.ctx_seq & (rows.ctx_seq - 1) == 0 and rows.lat_seq & (rows.lat_seq - 1) == 0
    nj = f // tn
    kern = functools.partial(_ffn_in_kernel, n_ctx_blocks=rows.tc // tm,
                             ctx_seq=rows.ctx_seq, lat_seq=rows.lat_seq)
    cb = conv_b.reshape(1, 2 * f)
    return pl.pallas_call(
        kern,
        grid=(t // tm, nj),
        in_specs=[pl.BlockSpec((tm, d), lambda i, j: (i, 0), pipeline_mode=pl.Buffered(1)),
                  pl.BlockSpec((d, tn), lambda i, j: (0, j)),
                  pl.BlockSpec((d, tn), lambda i, j: (0, j + nj)),
                  pl.BlockSpec((3, tn), lambda i, j: (0, j)),
                  pl.BlockSpec((3, tn), lambda i, j: (0, j + nj)),
                  pl.BlockSpec((1, tn), lambda i, j: (0, j)),
                  pl.BlockSpec((1, tn), lambda i, j: (0, j + nj))],
        out_specs=pl.BlockSpec((tm, tn), lambda i, j: (i, j)),
        out_shape=jax.ShapeDtypeStruct((t, f), BF16),
        compiler_params=_cp("parallel", "parallel"),
        name="ffn_in",
    )(h, w_in, w_in, conv_w, conv_w, cb, cb)


def rope_tables(n_lat, head_dim, tm):
    d2 = head_dim // 2
    half = d2 // 2
    lane = jnp.arange(LANE)
    freqs = ROPE_BASE ** (-jnp.arange(half, dtype=F32) / half)
    f = freqs[lane % half]
    n = jnp.arange(n_lat)
    pos = jnp.where(((lane // d2) % 2 == 0)[None, :], (n // GRID_W)[:, None], (n % GRID_W)[:, None])
    ang = pos.astype(F32) * f[None, :]
    sign = jnp.where((lane % d2) < half, -1.0, 1.0).astype(F32)
    cos = jnp.concatenate([jnp.ones((tm, LANE), F32), jnp.cos(ang)], axis=0)
    sin = jnp.concatenate([jnp.zeros((tm, LANE), F32), jnp.sin(ang) * sign[None, :]], axis=0)
    return cos, sin, half


def _rope(x, cos, sin, first):
    half_shift = first[1]
    partner = jnp.where(first[0], pltpu.roll(x, LANE - half_shift, axis=1),
                        pltpu.roll(x, half_shift, axis=1))
    return x * cos + partner * sin


def _first_half_mask(shape, half):
    lane = lax.broadcasted_iota(jnp.int32, shape, 1)
    return ((lane & (2 * half - 1)) < half, half)


def _table_spec(rows, tm):
    nctx = rows.tc // tm
    per = rows.lat_seq // tm
    return lambda i: (jnp.where(i < nctx, 0, 1 + (i - nctx) % per), 0)


def _prep_kernel(x_ref, cos_ref, sin_ref, g_ref, o_ref, *, plan, half):
    tm = x_ref.shape[0]
    cos, sin = cos_ref[...], sin_ref[...]
    first = _first_half_mask((tm, LANE), half)
    for c, (kind, gi) in enumerate(plan):
        x = x_ref[:, c * LANE:(c + 1) * LANE]
        if "norm" in kind:
            x = _rms(x, g_ref[gi:gi + 1, :])
        if "rope" in kind:
            x = _rope(x, cos, sin, first)
        o_ref[:, c * LANE:(c + 1) * LANE] = x.astype(o_ref.dtype)


def prep(x, plan, tables, gains, rows, tm, out_dtype, col_block=0, row_blocks=None):
    cos, sin, half = tables
    width = LANE * len(plan)
    nblk = rows.t // tm if row_blocks is None else row_blocks
    tspec = _table_spec(rows, tm)
    return pl.pallas_call(
        functools.partial(_prep_kernel, plan=tuple(plan), half=half),
        grid=(nblk,),
        in_specs=[pl.BlockSpec((tm, width), lambda i: (i, col_block)),
                  pl.BlockSpec((tm, LANE), tspec),
                  pl.BlockSpec((tm, LANE), tspec),
                  pl.BlockSpec(gains.shape, lambda i: (0, 0))],
        out_specs=pl.BlockSpec((tm, width), lambda i: (i, 0)),
        out_shape=jax.ShapeDtypeStruct((nblk * tm, width), out_dtype),
        compiler_params=_cp("parallel"),
        name="prep",
    )(x, cos, sin, gains)


def _mla_prep_kernel(x_ref, cos_ref, sin_ref, gq_ref, gkv_ref,
                     cq_ref, ckvf_ref, ckvb_ref, kpef_ref, kpeb_ref, *, ql, kvl, half):
    tm = x_ref.shape[0]
    cq_ref[...] = _rms(x_ref[:, :ql], gq_ref[...]).astype(BF16)
    ckv = _rms(x_ref[:, ql:ql + kvl], gkv_ref[...])
    ckvf_ref[...] = ckv
    ckvb_ref[...] = ckv.astype(BF16)
    kpe = x_ref[:, ql + kvl:ql + kvl + LANE]
    kpef_ref[...] = kpe
    kpeb_ref[...] = _rope(kpe, cos_ref[...], sin_ref[...],
                          _first_half_mask((tm, LANE), half)).astype(BF16)


def mla_prep(x, gq, gkv, tables, rows, tm):
    cos, sin, half = tables
    ql, kvl = gq.shape[0], gkv.shape[0]
    t, width = x.shape
    tspec = _table_spec(rows, tm)
    row = lambda i: (i, 0)
    return pl.pallas_call(
        functools.partial(_mla_prep_kernel, ql=ql, kvl=kvl, half=half),
        grid=(t // tm,),
        in_specs=[pl.BlockSpec((tm, width), row),
                  pl.BlockSpec((tm, LANE), tspec),
                  pl.BlockSpec((tm, LANE), tspec),
                  pl.BlockSpec((1, ql), lambda i: (0, 0)),
                  pl.BlockSpec((1, kvl), lambda i: (0, 0))],
        out_specs=[pl.BlockSpec((tm, ql), row), pl.BlockSpec((tm, kvl), row),
                   pl.BlockSpec((tm, kvl), row), pl.BlockSpec((tm, LANE), row),
                   pl.BlockSpec((tm, LANE), row)],
        out_shape=[jax.ShapeDtypeStruct((t, ql), BF16), jax.ShapeDtypeStruct((t, kvl), F32),
                   jax.ShapeDtypeStruct((t, kvl), BF16), jax.ShapeDtypeStruct((t, LANE), F32),
                   jax.ShapeDtypeStruct((t, LANE), BF16)],
        compiler_params=_cp("parallel"),
        name="mla_prep",
    )(x, cos, sin, gq.reshape(1, ql), gkv.reshape(1, kvl))


def _attn_kernel(*refs, scale, tq, n_ctx, n_lat, span, windowed, has_kpe, pair):
    refs = list(refs)
    sink_ref = refs.pop(0) if pair else None
    q_ref, k_ref, v_ref = refs[:3]
    kpe_ref = refs[3] if has_kpe else None
    o_ref = refs[-1]
    hp = pl.program_id(1)
    qi = pl.program_id(2)
    q = q_ref[...]

    if windowed:
        start = pl.multiple_of(jnp.clip(qi * tq - WINDOW, 0, n_lat - span), LANE)
        qpos = qi * tq + lax.broadcasted_iota(jnp.int32, (tq, span), 0)
        kpos = start + lax.broadcasted_iota(jnp.int32, (tq, span), 1)
        valid = jnp.abs(qpos - kpos) <= WINDOW
        parts = [(k_ref[0:n_ctx, :], v_ref[0:n_ctx, :], None),
                 (k_ref[pl.ds(n_ctx + start, span), :], v_ref[pl.ds(n_ctx + start, span), :], valid)]
    else:
        k = k_ref[...]
        if has_kpe:
            k = jnp.concatenate([k, kpe_ref[...]], axis=1)
        parts = [(k, v_ref[...], None)]

    def head(qh, sink):
        scores = []
        for k, _, valid in parts:
            s = lax.dot_general(qh, k, (((1,), (1,)), ((), ())), preferred_element_type=F32) * scale
            scores.append(s if valid is None else jnp.where(valid, s, NEG_INF))
        m = functools.reduce(jnp.maximum, [s.max(axis=-1, keepdims=True) for s in scores])
        if sink is not None:
            m = jnp.maximum(m, sink)
        den, o = 0.0, 0.0
        for s, (_, v, _) in zip(scores, parts):
            p = jnp.exp(s - m)
            den = den + p.sum(axis=-1, keepdims=True)
            o = o + jnp.dot(p.astype(BF16), v, preferred_element_type=F32)
        if sink is not None:
            den = den + jnp.exp(sink - m)
        return o / den

    if pair:
        lo = lax.broadcasted_iota(jnp.int32, q.shape, 1) < LANE // 2
        zero = jnp.zeros_like(q)
        oa = head(jnp.where(lo, q, zero), sink_ref[2 * hp])
        ob = head(jnp.where(lo, zero, q), sink_ref[2 * hp + 1])
        o = jnp.where(lax.broadcasted_iota(jnp.int32, oa.shape, 1) < LANE // 2, oa, ob)
    else:
        o = head(q, None)
    o_ref[...] = o.astype(o_ref.dtype)


def attention(q, k, v, kpe, sink, *, batch, n_q, n_k, q_row0, n_steps, q_width, q_col, k_col, v_col,
              scale, tq, windowed=False, n_ctx=0):
    tq = min(tq, n_q)
    nq = n_q // tq
    pair = sink is not None
    span = min(tq + 2 * WINDOW, n_k - n_ctx) if windowed else 0
    kern = functools.partial(_attn_kernel, scale=scale, tq=tq, n_ctx=n_ctx, n_lat=n_k - n_ctx,
                             span=span, windowed=windowed, has_kpe=kpe is not None, pair=pair)
    row0 = q_row0 // tq
    in_specs = [pl.BlockSpec((tq, q_width), lambda b, h, i: (row0 + b * nq + i, q_col(h))),
                pl.BlockSpec((None, n_k, LANE), lambda b, h, i: (b, 0, k_col(h))),
                pl.BlockSpec((None, n_k, LANE), lambda b, h, i: (b, 0, v_col(h)))]
    args = [q, k, v]
    if kpe is not None:
        in_specs.append(pl.BlockSpec((None, n_k, LANE), lambda b, h, i: (b, 0, 0)))
        args.append(kpe)
    if pair:
        in_specs.insert(0, pl.BlockSpec(memory_space=pltpu.SMEM))
        args.insert(0, sink)
    return pl.pallas_call(
        kern,
        grid=(batch, n_steps, nq),
        in_specs=in_specs,
        out_specs=pl.BlockSpec((tq, LANE), lambda b, h, i: (b * nq + i, h)),
        out_shape=jax.ShapeDtypeStruct((batch * n_q, n_steps * LANE), BF16),
        compiler_params=_cp("parallel", "parallel", "arbitrary"),
        name="attention",
    )(*args)


def kernel(x_prompt, x_sample, cache_mla_ckv, cache_mla_kpe, cache_gqa_k, cache_gqa_v, cache_swa_k,
           cache_swa_v, c, c_ctx, mod_w, mod_b, norm_g, ffn_w_in, ffn_conv_w, ffn_conv_b, ffn_w_out,
           mla_w_in, mla_q_norm_g, mla_kv_norm_g, mla_w_uq, mla_w_ukv, mla_w_o, gqa_w_qkv,
           gqa_q_norm_g, gqa_k_norm_g, gqa_w_o, swa_w_qkv, swa_sink, swa_w_o):
    bc, sc, d = x_prompt.shape
    bl, sl, _ = x_sample.shape
    depth = mod_w.shape[0]
    past = cache_mla_ckv.shape[2]
    rows = Rows(bc * sc, sc, bl * sl, sl)
    tc, t = rows.tc, rows.t
    tm_row = min(512, sc * bc, sl)
    tm_mm = min(1024, sc * bc, sl)
    tm_prep = min(256, sc * bc, sl)

    ql, kvl = mla_q_norm_g.shape[1], mla_kv_norm_g.shape[1]
    mla_rope = cache_mla_kpe.shape[3]
    mla_h = mla_w_o.shape[1] // LANE
    mla_qd = mla_w_uq.shape[2] // mla_h
    mla_nope = mla_qd - mla_rope
    assert mla_nope == LANE and mla_rope <= LANE and mla_w_ukv.shape[2] == mla_h * 2 * LANE
    mla_scale = float(mla_qd) ** -0.5
    gqa_kvh, gqa_hd = cache_gqa_k.shape[3], cache_gqa_k.shape[4]
    gqa_h = gqa_w_o.shape[1] // gqa_hd
    assert gqa_hd == LANE
    swa_kvh, swa_hd = cache_swa_k.shape[3], cache_swa_k.shape[4]
    swa_h = swa_sink.shape[1]
    swa_group = swa_h // swa_kvh
    assert 2 * swa_hd == LANE and swa_group % 2 == 0

    x = jnp.concatenate([x_prompt.reshape(tc, d), x_sample.reshape(bl * sl, d)], axis=0)
    cvec = jnp.concatenate([c_ctx[None], c, jnp.zeros((MOD_ROWS - 1 - bl, d), F32)], axis=0)
    mods = adaln(cvec, mod_w, mod_b).reshape(depth * MOD_ROWS * 6, 1, d)

    assert mla_rope == swa_hd
    tab64 = rope_tables(sl, swa_hd, tm_prep)
    tab128 = rope_tables(sl, LANE, tm_prep)
    no_gain = jnp.ones((1, LANE), F32)

    outs = dict(ckv=[], kpe=[], gk=[], gv=[], sk=[], sv=[])
    mm = functools.partial(matmul, tm=tm_mm)

    for i in range(depth):
        kind, j = i % 3, i // 3
        h = normmod(x, norm_g[i, 0], mods, i, 0, rows, tm_row)

        if kind == 0:
            w_in = jnp.pad(mla_w_in[j], ((0, 0), (0, ql + kvl + LANE - mla_w_in.shape[2]))).astype(BF16)
            proj = mm(h, w_in, F32, tn=w_in.shape[1], single_w=True, tm=min(512, tm_mm))
            cq, ckv_f, ckv_b, kpe_f, kpe_b = mla_prep(proj, mla_q_norm_g[j], mla_kv_norm_g[j],
                                                      tab64, rows, tm_prep)
            outs["ckv"].append(ckv_f[:tc].reshape(bc, sc, kvl))
            outs["kpe"].append(kpe_f[:tc, :mla_rope].reshape(bc, sc, mla_rope))
            w_uq = jnp.pad(mla_w_uq[j].reshape(ql, mla_h, mla_qd),
                           ((0, 0), (0, 0), (0, 2 * LANE - mla_qd))).reshape(ql, mla_h * 2 * LANE)
            q_f = mm(cq, w_uq.astype(BF16), F32, tn=1024)
            q = prep(q_f, [("cast", 0), ("rope", 0)] * mla_h, tab64, no_gain, rows, tm_prep, BF16)
            w_ukv = mla_w_ukv[j].astype(BF16)
            kv_c = mm(ckv_b[:tc], w_ukv, BF16, tn=1024).reshape(bc, sc, -1)
            kpe_c = kpe_b[:tc].reshape(bc, sc, LANE)
            a_c = attention(q, kv_c, kv_c, kpe_c, None, batch=bc, n_q=sc, n_k=sc, q_row0=0,
                            n_steps=mla_h, q_width=2 * LANE, q_col=lambda hh: hh,
                            k_col=lambda hh: 2 * hh, v_col=lambda hh: 2 * hh + 1,
                            scale=mla_scale, tq=512)
            ckv_l = jnp.concatenate([cache_mla_ckv[:, j].astype(BF16), ckv_b[tc:].reshape(bl, sl, kvl)],
                                    axis=1)
            kpe_cache = jnp.pad(cache_mla_kpe[:, j], ((0, 0), (0, 0), (0, LANE - mla_rope))).astype(BF16)
            kpe_l = jnp.concatenate([kpe_cache, kpe_b[tc:].reshape(bl, sl, LANE)], axis=1)
            kv_l = mm(ckv_l.reshape(bl * (past + sl), kvl), w_ukv, BF16, tn=1024,
                      tm=_row_tile(bl * (past + sl))).reshape(bl, past + sl, -1)
            a_l = attention(q, kv_l, kv_l, kpe_l, None, batch=bl, n_q=sl, n_k=past + sl, q_row0=tc,
                            n_steps=mla_h, q_width=2 * LANE, q_col=lambda hh: hh,
                            k_col=lambda hh: 2 * hh, v_col=lambda hh: 2 * hh + 1,
                            scale=mla_scale, tq=512)
            w_o = mla_w_o[j]
        elif kind == 1:
            nq, nkv = gqa_h, gqa_kvh
            qkv_f = mm(h, gqa_w_qkv[j].astype(BF16), F32, tn=1024)
            gains = jnp.stack([gqa_q_norm_g[j], gqa_k_norm_g[j]])
            plan = [("norm_rope", 0)] * nq + [("norm_rope", 1)] * nkv + [("cast", 0)] * nkv
            qkv = prep(qkv_f, plan, tab128, gains, rows, tm_prep, BF16)
            k_f = prep(qkv_f, [("norm", 1)] * nkv, tab128, gains, rows, tm_prep, F32,
                       col_block=nq // nkv, row_blocks=tc // tm_prep)
            outs["gk"].append(k_f.reshape(bc, sc, nkv, gqa_hd))
            outs["gv"].append(qkv_f[:tc, (nq + nkv) * LANE:].reshape(bc, sc, nkv, gqa_hd))
            group = nq // nkv
            k_c = qkv[:tc].reshape(bc, sc, -1)
            a_c = attention(qkv, k_c, k_c, None, None, batch=bc, n_q=sc, n_k=sc, q_row0=0,
                            n_steps=nq, q_width=LANE, q_col=lambda hh: hh,
                            k_col=lambda hh: nq + hh // group, v_col=lambda hh: nq + nkv + hh // group,
                            scale=float(gqa_hd) ** -0.5, tq=512)
            k_l = jnp.concatenate([cache_gqa_k[:, j].reshape(bl, past, nkv * LANE).astype(BF16),
                                   qkv[tc:, nq * LANE:(nq + nkv) * LANE].reshape(bl, sl, nkv * LANE)],
                                  axis=1)
            v_l = jnp.concatenate([cache_gqa_v[:, j].reshape(bl, past, nkv * LANE).astype(BF16),
                                   qkv[tc:, (nq + nkv) * LANE:].reshape(bl, sl, nkv * LANE)], axis=1)
            a_l = attention(qkv, k_l, v_l, None, None, batch=bl, n_q=sl, n_k=past + sl, q_row0=tc,
                            n_steps=nq, q_width=LANE, q_col=lambda hh: hh,
                            k_col=lambda hh: hh // group, v_col=lambda hh: hh // group,
                            scale=float(gqa_hd) ** -0.5, tq=512)
            w_o = gqa_w_o[j]
        else:
            nqb = swa_h // 2
            w = swa_w_qkv[j]
            wq, wk, wv = jnp.split(w, [swa_h * swa_hd, (swa_h + swa_kvh) * swa_hd], axis=1)
            dup = lambda m: jnp.tile(m.reshape(d, swa_kvh, 1, swa_hd), (1, 1, 2, 1)).reshape(d, -1)
            w_dup = jnp.concatenate([wq, dup(wk), dup(wv)], axis=1).astype(BF16)
            qkv_f = mm(h, w_dup, F32, tn=1024)
            plan = [("rope", 0)] * (nqb + swa_kvh) + [("cast", 0)] * swa_kvh
            qkv = prep(qkv_f, plan, tab64, no_gain, rows, tm_prep, BF16)
            kv_f = qkv_f[:tc, nqb * LANE:].reshape(bc, sc, 2, swa_kvh, 2, swa_hd)
            outs["sk"].append(kv_f[:, :, 0, :, 0])
            outs["sv"].append(kv_f[:, :, 1, :, 0])
            sink = swa_sink[j].astype(F32)
            half_group = swa_group // 2
            k_c = qkv[:tc].reshape(bc, sc, -1)
            a_c = attention(qkv, k_c, k_c, None, sink, batch=bc, n_q=sc, n_k=sc, q_row0=0,
                            n_steps=nqb, q_width=LANE, q_col=lambda hh: hh,
                            k_col=lambda hh: nqb + hh // half_group,
                            v_col=lambda hh: nqb + swa_kvh + hh // half_group,
                            scale=float(swa_hd) ** -0.5, tq=512)
            dup_cache = lambda m: jnp.tile(m[:, :, :, None, :], (1, 1, 1, 2, 1)).reshape(
                bl, past, swa_kvh * LANE).astype(BF16)
            k_l = jnp.concatenate([dup_cache(cache_swa_k[:, j]),
                                   qkv[tc:, nqb * LANE:(nqb + swa_kvh) * LANE].reshape(bl, sl, -1)], axis=1)
            v_l = jnp.concatenate([dup_cache(cache_swa_v[:, j]),
                                   qkv[tc:, (nqb + swa_kvh) * LANE:].reshape(bl, sl, -1)], axis=1)
            a_l = attention(qkv, k_l, v_l, None, sink, batch=bl, n_q=sl, n_k=past + sl, q_row0=tc,
                            n_steps=nqb, q_width=LANE, q_col=lambda hh: hh,
                            k_col=lambda hh: hh // half_group, v_col=lambda hh: hh // half_group,
                            scale=float(swa_hd) ** -0.5, tq=256, windowed=True, n_ctx=past)
            w_o = swa_w_o[j]

        a = mm(jnp.concatenate([a_c, a_l], axis=0), w_o.astype(BF16), F32, tn=1024)
        x = resid(x, a, norm_g[i, 1], mods, i, 2, rows, tm_row)

        h = normmod(x, norm_g[i, 2], mods, i, 3, rows, tm_row)
        gated = ffn_in(h, ffn_w_in[i].astype(BF16), ffn_conv_w[i], ffn_conv_b[i], rows,
                       tm=rows.lat_seq, tn=256)
        y = mm(gated, ffn_w_out[i].astype(BF16), F32, tn=256, single_x=True)
        x = resid(x, y, norm_g[i, 3], mods, i, 5, rows, tm_row)

    stack = lambda xs: jnp.stack(xs, axis=1)
    return (x[:tc].reshape(bc, sc, d), x[tc:].reshape(bl, sl, d),
            stack(outs["ckv"]), stack(outs["kpe"]), stack(outs["gk"]), stack(outs["gv"]),
            stack(outs["sk"]), stack(outs["sv"]))


def _row_tile(m):
    for tm in (1024, 512, 256, 128, 64, 32, 16, 8):
        if m % tm == 0:
            return tm
    return m
```

```python
import functools

import jax
import jax.numpy as jnp
from jax import lax
from jax.experimental import pallas as pl
from jax.experimental.pallas import tpu as pltpu

F32 = jnp.float32
BF16 = jnp.bfloat16

EPS = 1e-6
NEG_INF = -1e30
LOG2E = 1.4426950408889634
ROPE_BASE = 10000.0
GRID_W = 64
WINDOW = 128
LANE = 128
MOD_ROWS = 16
VMEM_LIMIT = 56 * 1024 * 1024


def _cp(*sem):
    return pltpu.CompilerParams(dimension_semantics=sem, vmem_limit_bytes=VMEM_LIMIT)


def _rms(x, g):
    ms = jnp.mean(x * x, axis=-1, keepdims=True)
    return x * lax.rsqrt(ms + EPS) * g


def _adaln_kernel(c_ref, w_ref, b_ref, o_ref):
    c = c_ref[...]
    s = c / (1.0 + jnp.exp(-c))
    o_ref[...] = jnp.dot(s.astype(BF16), w_ref[...].astype(BF16),
                         preferred_element_type=F32) + b_ref[...]


def adaln(cvec, mod_w, mod_b):
    n_layers, d, n = mod_w.shape
    tn = 512 if n % 512 == 0 else n
    return pl.pallas_call(
        _adaln_kernel,
        grid=(n_layers, n // tn),
        in_specs=[pl.BlockSpec((MOD_ROWS, d), lambda l, j: (0, 0)),
                  pl.BlockSpec((None, d, tn), lambda l, j: (l, 0, j)),
                  pl.BlockSpec((None, 1, tn), lambda l, j: (l, 0, j))],
        out_specs=pl.BlockSpec((None, MOD_ROWS, tn), lambda l, j: (l, 0, j)),
        out_shape=jax.ShapeDtypeStruct((n_layers, MOD_ROWS, n), F32),
        compiler_params=_cp("parallel", "parallel"),
        name="adaln",
    )(cvec, mod_w, mod_b.reshape(n_layers, 1, n))


class Rows:
    def __init__(self, n_ctx_rows, ctx_seq, n_lat_rows, lat_seq):
        self.tc, self.ctx_seq, self.tl, self.lat_seq = n_ctx_rows, ctx_seq, n_lat_rows, lat_seq
        self.t = n_ctx_rows + n_lat_rows

    def mod_row(self, i, tm):
        nctx = self.tc // tm
        per = self.lat_seq // tm
        return jnp.where(i < nctx, 0, 1 + (i - nctx) // per)


def _mod_spec(rows, tm, layer, comp, d):
    base = layer * MOD_ROWS * 6 + comp
    return pl.BlockSpec((None, 1, d), lambda i: (base + rows.mod_row(i, tm) * 6, 0, 0))


def _gain_spec(layer, which, d):
    return pl.BlockSpec((None, None, 1, d), lambda i: (layer, which, 0, 0))


def _normmod_kernel(x_ref, g_ref, sh_ref, sc_ref, o_ref):
    y = _rms(x_ref[...], g_ref[...])
    o_ref[...] = (y * (1.0 + sc_ref[...]) + sh_ref[...]).astype(o_ref.dtype)


def normmod(x, norm_g4, mods, layer, which, comp, rows, tm):
    t, d = x.shape
    row = pl.BlockSpec((tm, d), lambda i: (i, 0))
    return pl.pallas_call(
        _normmod_kernel,
        grid=(t // tm,),
        in_specs=[row, _gain_spec(layer, which, d),
                  _mod_spec(rows, tm, layer, comp, d), _mod_spec(rows, tm, layer, comp + 1, d)],
        out_specs=row,
        out_shape=jax.ShapeDtypeStruct((t, d), BF16),
        compiler_params=_cp("parallel"),
        name="normmod",
    )(x, norm_g4, mods, mods)


def _resid_kernel(x_ref, a_ref, g_ref, gate_ref, o_ref):
    o_ref[...] = x_ref[...] + gate_ref[...] * _rms(a_ref[...], g_ref[...])


def resid(x, a, norm_g4, mods, layer, which, comp, rows, tm):
    t, d = x.shape
    row = pl.BlockSpec((tm, d), lambda i: (i, 0))
    return pl.pallas_call(
        _resid_kernel,
        grid=(t // tm,),
        in_specs=[row, row, _gain_spec(layer, which, d), _mod_spec(rows, tm, layer, comp, d)],
        out_specs=row,
        out_shape=jax.ShapeDtypeStruct((t, d), F32),
        compiler_params=_cp("parallel"),
        name="resid",
    )(x, a, norm_g4, mods)


def _resid_norm_kernel(x_ref, a_ref, g_ref, gate_ref, g2_ref, sh_ref, sc_ref, o_ref, h_ref):
    x = x_ref[...] + gate_ref[...] * _rms(a_ref[...], g_ref[...])
    o_ref[...] = x
    h_ref[...] = (_rms(x, g2_ref[...]) * (1.0 + sc_ref[...]) + sh_ref[...]).astype(h_ref.dtype)


def resid_norm(x, a, norm_g4, mods, layer, which, comp, nxt_layer, nxt_which, nxt_comp, rows, tm):
    t, d = x.shape
    row = pl.BlockSpec((tm, d), lambda i: (i, 0))
    return pl.pallas_call(
        _resid_norm_kernel,
        grid=(t // tm,),
        in_specs=[row, row, _gain_spec(layer, which, d), _mod_spec(rows, tm, layer, comp, d),
                  _gain_spec(nxt_layer, nxt_which, d),
                  _mod_spec(rows, tm, nxt_layer, nxt_comp, d),
                  _mod_spec(rows, tm, nxt_layer, nxt_comp + 1, d)],
        out_specs=[row, row],
        out_shape=[jax.ShapeDtypeStruct((t, d), F32), jax.ShapeDtypeStruct((t, d), BF16)],
        compiler_params=_cp("parallel"),
        name="resid_norm",
    )(x, a, norm_g4, mods, norm_g4, mods, mods)


def _mm_kernel(x_ref, w_ref, o_ref):
    o_ref[...] = jnp.dot(x_ref[...], w_ref[...], preferred_element_type=F32).astype(o_ref.dtype)


def _mm2_kernel(xa_ref, xb_ref, w_ref, o_ref, *, na):
    @pl.when(pl.program_id(0) < na)
    def _():
        o_ref[...] = jnp.dot(xa_ref[...], w_ref[...], preferred_element_type=F32).astype(o_ref.dtype)

    @pl.when(pl.program_id(0) >= na)
    def _():
        o_ref[...] = jnp.dot(xb_ref[...], w_ref[...], preferred_element_type=F32).astype(o_ref.dtype)


def _w_spec(w, layer, k, tn, **mode):
    if w.ndim == 3:
        return pl.BlockSpec((None, k, tn), lambda i, j: (layer, 0, j), **mode)
    return pl.BlockSpec((k, tn), lambda i, j: (0, j), **mode)


def matmul(x, w, out_dtype, tm, tn, layer=0, single_x=False, single_w=False):
    m, k = x.shape
    n = w.shape[-1]
    tm, tn = min(tm, m), min(tn, n)
    assert m % tm == 0 and n % tn == 0, (m, n, tm, tn)
    x_mode = dict(pipeline_mode=pl.Buffered(1)) if single_x else {}
    w_mode = dict(pipeline_mode=pl.Buffered(1)) if single_w else {}
    return pl.pallas_call(
        _mm_kernel,
        grid=(m // tm, n // tn),
        in_specs=[pl.BlockSpec((tm, k), lambda i, j: (i, 0), **x_mode), _w_spec(w, layer, k, tn, **w_mode)],
        out_specs=pl.BlockSpec((tm, tn), lambda i, j: (i, j)),
        out_shape=jax.ShapeDtypeStruct((m, n), out_dtype),
        compiler_params=_cp("parallel", "parallel"),
        name="matmul",
    )(x, w)


def matmul2(xa, xb, w, out_dtype, tm, tn, layer=0):
    ma, k = xa.shape
    mb = xb.shape[0]
    n = w.shape[-1]
    tn = min(tn, n)
    assert ma % tm == 0 and mb % tm == 0 and n % tn == 0, (ma, mb, n, tm, tn)
    na = ma // tm
    return pl.pallas_call(
        functools.partial(_mm2_kernel, na=na),
        grid=((ma + mb) // tm, n // tn),
        in_specs=[pl.BlockSpec((tm, k), lambda i, j: (jnp.minimum(i, na - 1), 0)),
                  pl.BlockSpec((tm, k), lambda i, j: (jnp.maximum(i - na, 0), 0)),
                  _w_spec(w, layer, k, tn)],
        out_specs=pl.BlockSpec((tm, tn), lambda i, j: (i, j)),
        out_shape=jax.ShapeDtypeStruct((ma + mb, n), out_dtype),
        compiler_params=_cp("parallel", "parallel"),
        name="matmul2",
    )(xa, xb, w)


def _ffn_in_kernel(h_ref, wg_ref, wv_ref, cwg_ref, cwv_ref, cbg_ref, cbv_ref, o_ref, *,
                   n_ctx_blocks, ctx_seq, lat_seq, ch):
    tm, tn = o_ref.shape
    nch = tm // ch
    seq = jnp.where(pl.program_id(0) < n_ctx_blocks, ctx_seq, lat_seq)
    row = lax.broadcasted_iota(jnp.int32, (ch, tn), 0)
    wg = wg_ref[...].astype(BF16)
    wv = wv_ref[...].astype(BF16)
    ug = [jnp.dot(h_ref[r * ch:(r + 1) * ch, :], wg, preferred_element_type=F32) for r in range(nch)]
    uv = [jnp.dot(h_ref[r * ch:(r + 1) * ch, :], wv, preferred_element_type=F32) for r in range(nch)]

    def conv(us, r, cw_ref, cb_ref):
        u = us[r]
        pos = (row + r * ch) & (seq - 1)
        prev = pltpu.roll(u, 1, axis=0)
        if r > 0:
            prev = jnp.where(row == 0, us[r - 1][ch - 1:ch, :], prev)
        prev = jnp.where(pos == 0, 0.0, prev)
        nxt = pltpu.roll(u, ch - 1, axis=0)
        if r < nch - 1:
            nxt = jnp.where(row == ch - 1, us[r + 1][0:1, :], nxt)
        nxt = jnp.where(pos == seq - 1, 0.0, nxt)
        return prev * cw_ref[0:1, :] + u * cw_ref[1:2, :] + nxt * cw_ref[2:3, :] + cb_ref[...]

    for r in range(nch):
        gate = conv(ug, r, cwg_ref, cbg_ref)
        val = conv(uv, r, cwv_ref, cbv_ref)
        o_ref[r * ch:(r + 1) * ch, :] = (gate / (1.0 + jnp.exp(-gate)) * val).astype(o_ref.dtype)


def ffn_in(h, w_in, conv_w, conv_b, layer, rows, tm, tn, ch):
    t, d = h.shape
    f = w_in.shape[2] // 2
    assert f % tn == 0 and rows.tc % tm == 0 and tm % rows.ctx_seq == 0 and tm == rows.lat_seq
    assert rows.ctx_seq & (rows.ctx_seq - 1) == 0 and rows.lat_seq & (rows.lat_seq - 1) == 0
    assert tm % ch == 0
    nj = f // tn
    kern = functools.partial(_ffn_in_kernel, n_ctx_blocks=rows.tc // tm,
                             ctx_seq=rows.ctx_seq, lat_seq=rows.lat_seq, ch=ch)
    cb = conv_b.reshape(conv_b.shape[0], 1, 2 * f)
    return pl.pallas_call(
        kern,
        grid=(t // tm, nj),
        in_specs=[pl.BlockSpec((tm, d), lambda i, j: (i, 0), pipeline_mode=pl.Buffered(1)),
                  pl.BlockSpec((None, d, tn), lambda i, j: (layer, 0, j)),
                  pl.BlockSpec((None, d, tn), lambda i, j: (layer, 0, j + nj)),
                  pl.BlockSpec((None, 3, tn), lambda i, j: (layer, 0, j)),
                  pl.BlockSpec((None, 3, tn), lambda i, j: (layer, 0, j + nj)),
                  pl.BlockSpec((None, 1, tn), lambda i, j: (layer, 0, j)),
                  pl.BlockSpec((None, 1, tn), lambda i, j: (layer, 0, j + nj))],
        out_specs=pl.BlockSpec((tm, tn), lambda i, j: (i, j)),
        out_shape=jax.ShapeDtypeStruct((t, f), BF16),
        compiler_params=_cp("parallel", "parallel"),
        name="ffn_in",
    )(h, w_in, w_in, conv_w, conv_w, cb, cb)


def rope_tables(n_lat, head_dim, tm):
    d2 = head_dim // 2
    half = d2 // 2
    lane = jnp.arange(LANE)
    freqs = ROPE_BASE ** (-jnp.arange(half, dtype=F32) / half)
    f = freqs[lane % half]
    n = jnp.arange(n_lat)
    pos = jnp.where(((lane // d2) % 2 == 0)[None, :], (n // GRID_W)[:, None], (n % GRID_W)[:, None])
    ang = pos.astype(F32) * f[None, :]
    sign = jnp.where((lane % d2) < half, -1.0, 1.0).astype(F32)
    cos = jnp.concatenate([jnp.ones((tm, LANE), F32), jnp.cos(ang)], axis=0)
    sin = jnp.concatenate([jnp.zeros((tm, LANE), F32), jnp.sin(ang) * sign[None, :]], axis=0)
    return cos, sin, half


def _rope(x, cos, sin, half):
    lane = lax.broadcasted_iota(jnp.int32, x.shape, 1)
    partner = jnp.where((lane & (2 * half - 1)) < half, pltpu.roll(x, LANE - half, axis=1),
                        pltpu.roll(x, half, axis=1))
    return x * cos + partner * sin


def _table_spec(rows, tm):
    nctx = rows.tc // tm
    per = rows.lat_seq // tm
    return lambda i: (jnp.where(i < nctx, 0, 1 + (i - nctx) % per), 0)


def _prep_kernel(x_ref, cos_ref, sin_ref, g_ref, o_ref, *, plan, half):
    cos, sin = cos_ref[...], sin_ref[...]
    for c, (kind, gi) in enumerate(plan):
        x = x_ref[:, c * LANE:(c + 1) * LANE]
        if "norm" in kind:
            x = _rms(x, g_ref[gi:gi + 1, :])
        if "rope" in kind:
            x = _rope(x, cos, sin, half)
        o_ref[:, c * LANE:(c + 1) * LANE] = x.astype(o_ref.dtype)


def prep(x, plan, tables, gains, rows, tm, out_dtype, col_block=0, row_blocks=None):
    cos, sin, half = tables
    width = LANE * len(plan)
    nblk = rows.t // tm if row_blocks is None else row_blocks
    tspec = _table_spec(rows, tm)
    return pl.pallas_call(
        functools.partial(_prep_kernel, plan=tuple(plan), half=half),
        grid=(nblk,),
        in_specs=[pl.BlockSpec((tm, width), lambda i: (i, col_block)),
                  pl.BlockSpec((tm, LANE), tspec),
                  pl.BlockSpec((tm, LANE), tspec),
                  pl.BlockSpec(gains.shape, lambda i: (0, 0))],
        out_specs=pl.BlockSpec((tm, width), lambda i: (i, 0)),
        out_shape=jax.ShapeDtypeStruct((nblk * tm, width), out_dtype),
        compiler_params=_cp("parallel"),
        name="prep",
    )(x, cos, sin, gains)


def _mla_prep_kernel(x_ref, cos_ref, sin_ref, gq_ref, gkv_ref,
                     cq_ref, ckvf_ref, ckvb_ref, kpef_ref, kpeb_ref, *, ql, kvl, half):
    cq_ref[...] = _rms(x_ref[:, :ql], gq_ref[...]).astype(BF16)
    ckv = _rms(x_ref[:, ql:ql + kvl], gkv_ref[...])
    ckvf_ref[...] = ckv
    ckvb_ref[...] = ckv.astype(BF16)
    kpe = x_ref[:, ql + kvl:ql + kvl + LANE]
    kpef_ref[...] = kpe
    kpeb_ref[...] = _rope(kpe, cos_ref[...], sin_ref[...], half).astype(BF16)


def mla_prep(x, gq, gkv, tables, rows, tm):
    cos, sin, half = tables
    ql, kvl = gq.shape[0], gkv.shape[0]
    t, width = x.shape
    tspec = _table_spec(rows, tm)
    row = lambda i: (i, 0)
    return pl.pallas_call(
        functools.partial(_mla_prep_kernel, ql=ql, kvl=kvl, half=half),
        grid=(t // tm,),
        in_specs=[pl.BlockSpec((tm, width), row),
                  pl.BlockSpec((tm, LANE), tspec),
                  pl.BlockSpec((tm, LANE), tspec),
                  pl.BlockSpec((1, ql), lambda i: (0, 0)),
                  pl.BlockSpec((1, kvl), lambda i: (0, 0))],
        out_specs=[pl.BlockSpec((tm, ql), row), pl.BlockSpec((tm, kvl), row),
                   pl.BlockSpec((tm, kvl), row), pl.BlockSpec((tm, LANE), row),
                   pl.BlockSpec((tm, LANE), row)],
        out_shape=[jax.ShapeDtypeStruct((t, ql), BF16), jax.ShapeDtypeStruct((t, kvl), F32),
                   jax.ShapeDtypeStruct((t, kvl), BF16), jax.ShapeDtypeStruct((t, LANE), F32),
                   jax.ShapeDtypeStruct((t, LANE), BF16)],
        compiler_params=_cp("parallel"),
        name="mla_prep",
    )(x, cos, sin, gq.reshape(1, ql), gkv.reshape(1, kvl))


def _attn_kernel(*refs, c, tq, n_ctx, n_lat, span, windowed, mode, hps):
    refs = list(refs)
    pair = mode == "pair"
    sink_ref = refs.pop(0) if pair else None
    o_ref = refs.pop()
    if mode == "mla":
        q_ref, kv_ref, kpe_ref = refs
    else:
        q_ref, k_ref, v_ref = refs
    hg = pl.program_id(1)
    qi = pl.program_id(2)

    if windowed:
        start = pl.multiple_of(jnp.clip(qi * tq - WINDOW, 0, n_lat - span), LANE)
        qpos = qi * tq + lax.broadcasted_iota(jnp.int32, (tq, span), 0)
        kpos = start + lax.broadcasted_iota(jnp.int32, (tq, span), 1)
        valid = jnp.abs(qpos - kpos) <= WINDOW
        key_rows = [(slice(0, n_ctx), None), (pl.ds(n_ctx + start, span), valid)]
    else:
        key_rows = [(slice(None), None)]

    def softmax_pv(q, ks, vs, sink):
        scores = []
        for k, (_, valid) in zip(ks, key_rows):
            s = lax.dot_general(q, k, (((1,), (1,)), ((), ())), preferred_element_type=F32)
            scores.append(s if valid is None else jnp.where(valid, s, NEG_INF))
        m = functools.reduce(jnp.maximum, [s.max(axis=-1, keepdims=True) for s in scores]) * c
        if sink is not None:
            m = jnp.maximum(m, sink * LOG2E)
        o = 0.0
        for s, v in zip(scores, vs):
            o = o + jnp.dot(jnp.exp2(s * c - m).astype(BF16), v, preferred_element_type=F32)
        return o, m

    if pair:
        lo_k = lax.broadcasted_iota(jnp.int32, (1, LANE), 1) < LANE // 2
        lo_q = lax.broadcasted_iota(jnp.int32, (tq, LANE), 1) < LANE // 2
        ks = [k_ref[r, :] for r, _ in key_rows]
        one = jnp.ones((), BF16)
        va = [jnp.where(lo_k, v_ref[r, :], one) for r, _ in key_rows]
        vb = [jnp.where(lo_k, one, v_ref[r, :]) for r, _ in key_rows]
        for t in range(hps):
            q = q_ref[:, t * LANE:(t + 1) * LANE]
            zero = jnp.zeros_like(q)
            head = (hg * hps + t) * 2
            sa, sb = sink_ref[head], sink_ref[head + 1]
            oa, ma = softmax_pv(jnp.where(lo_q, q, zero), ks, va, sa)
            ob, mb = softmax_pv(jnp.where(lo_q, zero, q), ks, vb, sb)
            da = pltpu.roll(oa, LANE // 2, axis=1) + jnp.exp2(sa * LOG2E - ma)
            db = pltpu.roll(ob, LANE // 2, axis=1) + jnp.exp2(sb * LOG2E - mb)
            o_ref[:, t * LANE:(t + 1) * LANE] = jnp.where(lo_q, oa / da, ob / db).astype(o_ref.dtype)
    else:
        if mode == "mla":
            kpe = kpe_ref[...]
        else:
            k_shared = k_ref[...]
            v_shared = jnp.concatenate([v_ref[...], jnp.ones(v_ref.shape, BF16)], axis=1)
        for t in range(hps):
            if mode == "mla":
                q = q_ref[:, t * 2 * LANE:(t + 1) * 2 * LANE]
                k = jnp.concatenate([kv_ref[:, t * 2 * LANE:t * 2 * LANE + LANE], kpe], axis=1)
                v = jnp.concatenate([kv_ref[:, t * 2 * LANE + LANE:(t + 1) * 2 * LANE],
                                     jnp.ones(kpe.shape, BF16)], axis=1)
            else:
                q = q_ref[:, t * LANE:(t + 1) * LANE]
                k, v = k_shared, v_shared
            o, _ = softmax_pv(q, [k], [v], None)
            o_ref[:, t * LANE:(t + 1) * LANE] = (o[:, :LANE] / o[:, LANE:]).astype(o_ref.dtype)


def attention(q, k, v, kpe, sink, *, mode, batch, n_q, n_k, q_row0, n_groups, hps, k_col0, v_col0,
              scale, tq, windowed=False, n_ctx=0):
    tq = min(tq, n_q)
    nq = n_q // tq
    span = min(tq + 2 * WINDOW, n_k - n_ctx) if windowed else 0
    kern = functools.partial(_attn_kernel, c=scale * LOG2E, tq=tq, n_ctx=n_ctx, n_lat=n_k - n_ctx,
                             span=span, windowed=windowed, mode=mode, hps=hps)
    row0 = q_row0 // tq
    qw = hps * (2 * LANE if mode == "mla" else LANE)
    in_specs = [pl.BlockSpec((tq, qw), lambda b, g, i: (row0 + b * nq + i, g))]
    if mode == "mla":
        in_specs += [pl.BlockSpec((None, n_k, qw), lambda b, g, i: (b, 0, g)),
                     pl.BlockSpec((None, n_k, LANE), lambda b, g, i: (b, 0, 0))]
        args = [q, k, kpe]
    else:
        in_specs += [pl.BlockSpec((None, n_k, LANE), lambda b, g, i: (b, 0, k_col0 + g)),
                     pl.BlockSpec((None, n_k, LANE), lambda b, g, i: (b, 0, v_col0 + g))]
        args = [q, k, v]
    if mode == "pair":
        in_specs.insert(0, pl.BlockSpec(memory_space=pltpu.SMEM))
        args.insert(0, sink)
    return pl.pallas_call(
        kern,
        grid=(batch, n_groups, nq),
        in_specs=in_specs,
        out_specs=pl.BlockSpec((tq, hps * LANE), lambda b, g, i: (b * nq + i, g)),
        out_shape=jax.ShapeDtypeStruct((batch * n_q, n_groups * hps * LANE), BF16),
        compiler_params=_cp("parallel", "parallel", "arbitrary"),
        name="attention",
    )(*args)


def kernel(x_prompt, x_sample, cache_mla_ckv, cache_mla_kpe, cache_gqa_k, cache_gqa_v, cache_swa_k,
           cache_swa_v, c, c_ctx, mod_w, mod_b, norm_g, ffn_w_in, ffn_conv_w, ffn_conv_b, ffn_w_out,
           mla_w_in, mla_q_norm_g, mla_kv_norm_g, mla_w_uq, mla_w_ukv, mla_w_o, gqa_w_qkv,
           gqa_q_norm_g, gqa_k_norm_g, gqa_w_o, swa_w_qkv, swa_sink, swa_w_o):
    bc, sc, d = x_prompt.shape
    bl, sl, _ = x_sample.shape
    depth = mod_w.shape[0]
    past = cache_mla_ckv.shape[2]
    rows = Rows(bc * sc, sc, bl * sl, sl)
    tc = rows.tc
    tm_row = min(256, tc, sl)
    tm_mm = min(1024, tc, sl)
    tm_prep = min(256, tc, sl)
    ffn_ch = min(512, sl // 2)
    tq = 256

    ql, kvl = mla_q_norm_g.shape[1], mla_kv_norm_g.shape[1]
    mla_rope = cache_mla_kpe.shape[3]
    mla_h = mla_w_o.shape[1] // LANE
    mla_qd = mla_w_uq.shape[2] // mla_h
    assert mla_qd - mla_rope == LANE and mla_w_ukv.shape[2] == mla_h * 2 * LANE
    mla_scale = float(mla_qd) ** -0.5
    mla_hps = 4 if mla_h % 4 == 0 else 1
    gqa_kvh, gqa_hd = cache_gqa_k.shape[3], cache_gqa_k.shape[4]
    gqa_h = gqa_w_o.shape[1] // gqa_hd
    gqa_group = gqa_h // gqa_kvh
    assert gqa_hd == LANE
    swa_kvh, swa_hd = cache_swa_k.shape[3], cache_swa_k.shape[4]
    swa_h = swa_sink.shape[1]
    swa_group = swa_h // swa_kvh
    assert 2 * swa_hd == LANE and swa_group % 2 == 0 and mla_rope == swa_hd

    x = jnp.concatenate([x_prompt.reshape(tc, d), x_sample.reshape(bl * sl, d)], axis=0)
    cvec = jnp.concatenate([c_ctx[None], c, jnp.zeros((MOD_ROWS - 1 - bl, d), F32)], axis=0)
    mods = adaln(cvec, mod_w, mod_b).reshape(depth * MOD_ROWS * 6, 1, d)
    norm_g4 = norm_g.reshape(depth, 4, 1, d)

    tab64 = rope_tables(sl, swa_hd, tm_prep)
    tab128 = rope_tables(sl, LANE, tm_prep)
    no_gain = jnp.ones((1, LANE), F32)
    ffn_w_out_b = ffn_w_out.astype(BF16)
    mla_w_o_b, gqa_w_o_b, swa_w_o_b = mla_w_o.astype(BF16), gqa_w_o.astype(BF16), swa_w_o.astype(BF16)
    gqa_w_qkv_b, mla_w_ukv_b = gqa_w_qkv.astype(BF16), mla_w_ukv.astype(BF16)

    outs = dict(ckv=[], kpe=[], gk=[], gv=[], sk=[], sv=[])
    mm = functools.partial(matmul, tm=tm_mm)

    h = normmod(x, norm_g4, mods, 0, 0, 0, rows, tm_row)
    for i in range(depth):
        kind, j = i % 3, i // 3

        if kind == 0:
            w_in = jnp.pad(mla_w_in[j], ((0, 0), (0, ql + kvl + LANE - mla_w_in.shape[2]))).astype(BF16)
            proj = mm(h, w_in, F32, tn=w_in.shape[1], single_w=True, tm=min(512, tm_mm))
            cq, ckv_f, ckv_b, kpe_f, kpe_b = mla_prep(proj, mla_q_norm_g[j], mla_kv_norm_g[j],
                                                      tab64, rows, tm_prep)
            outs["ckv"].append(ckv_f[:tc].reshape(bc, sc, kvl))
            outs["kpe"].append(kpe_f[:tc, :mla_rope].reshape(bc, sc, mla_rope))
            w_uq = jnp.pad(mla_w_uq[j].reshape(ql, mla_h, mla_qd),
                           ((0, 0), (0, 0), (0, 2 * LANE - mla_qd))).reshape(ql, mla_h * 2 * LANE)
            q_f = mm(cq, w_uq.astype(BF16), F32, tn=1024)
            q = prep(q_f, [("cast", 0), ("rope", 0)] * mla_h, tab64, no_gain, rows, tm_prep, BF16)
            kv_c = mm(ckv_b[:tc], mla_w_ukv_b, BF16, tn=1024, layer=j).reshape(bc, sc, -1)
            kpe_c = kpe_b[:tc].reshape(bc, sc, LANE)
            mla_args = dict(mode="mla", n_groups=mla_h // mla_hps, hps=mla_hps, k_col0=0, v_col0=0,
                            scale=mla_scale, tq=tq)
            a_c = attention(q, kv_c, None, kpe_c, None, batch=bc, n_q=sc, n_k=sc, q_row0=0, **mla_args)
            ckv_l = jnp.concatenate([cache_mla_ckv[:, j].astype(BF16), ckv_b[tc:].reshape(bl, sl, kvl)],
                                    axis=1)
            kpe_cache = jnp.pad(cache_mla_kpe[:, j], ((0, 0), (0, 0), (0, LANE - mla_rope))).astype(BF16)
            kpe_l = jnp.concatenate([kpe_cache, kpe_b[tc:].reshape(bl, sl, LANE)], axis=1)
            kv_l = mm(ckv_l.reshape(bl * (past + sl), kvl), mla_w_ukv_b, BF16, tn=1024, layer=j,
                      tm=_row_tile(bl * (past + sl))).reshape(bl, past + sl, -1)
            a_l = attention(q, kv_l, None, kpe_l, None, batch=bl, n_q=sl, n_k=past + sl, q_row0=tc,
                            **mla_args)
            w_o = mla_w_o_b
        elif kind == 1:
            nq, nkv = gqa_h, gqa_kvh
            qkv_f = mm(h, gqa_w_qkv_b, F32, tn=1024, layer=j)
            gains = jnp.stack([gqa_q_norm_g[j], gqa_k_norm_g[j]])
            plan = [("norm_rope", 0)] * nq + [("norm_rope", 1)] * nkv + [("cast", 0)] * nkv
            qkv = prep(qkv_f, plan, tab128, gains, rows, tm_prep, BF16)
            k_f = prep(qkv_f, [("norm", 1)] * nkv, tab128, gains, rows, tm_prep, F32,
                       col_block=nq // nkv, row_blocks=tc // tm_prep)
            outs["gk"].append(k_f.reshape(bc, sc, nkv, gqa_hd))
            outs["gv"].append(qkv_f[:tc, (nq + nkv) * LANE:].reshape(bc, sc, nkv, gqa_hd))
            gqa_args = dict(mode="gqa", n_groups=nkv, hps=gqa_group, scale=float(gqa_hd) ** -0.5, tq=tq)
            k_c = qkv[:tc].reshape(bc, sc, -1)
            a_c = attention(qkv, k_c, k_c, None, None, batch=bc, n_q=sc, n_k=sc, q_row0=0,
                            k_col0=nq, v_col0=nq + nkv, **gqa_args)
            k_l = jnp.concatenate([cache_gqa_k[:, j].reshape(bl, past, nkv * LANE).astype(BF16),
                                   qkv[tc:, nq * LANE:(nq + nkv) * LANE].reshape(bl, sl, nkv * LANE)],
                                  axis=1)
            v_l = jnp.concatenate([cache_gqa_v[:, j].reshape(bl, past, nkv * LANE).astype(BF16),
                                   qkv[tc:, (nq + nkv) * LANE:].reshape(bl, sl, nkv * LANE)], axis=1)
            a_l = attention(qkv, k_l, v_l, None, None, batch=bl, n_q=sl, n_k=past + sl, q_row0=tc,
                            k_col0=0, v_col0=0, **gqa_args)
            w_o = gqa_w_o_b
        else:
            nqb = swa_h // 2
            wq, wk, wv = jnp.split(swa_w_qkv[j], [swa_h * swa_hd, (swa_h + swa_kvh) * swa_hd], axis=1)
            dup = lambda m: jnp.tile(m.reshape(d, swa_kvh, 1, swa_hd), (1, 1, 2, 1)).reshape(d, -1)
            w_dup = jnp.concatenate([wq, dup(wk), dup(wv)], axis=1).astype(BF16)
            qkv_f = mm(h, w_dup, F32, tn=1024)
            plan = [("rope", 0)] * (nqb + swa_kvh) + [("cast", 0)] * swa_kvh
            qkv = prep(qkv_f, plan, tab64, no_gain, rows, tm_prep, BF16)
            kv_f = qkv_f[:tc, nqb * LANE:].reshape(bc, sc, 2, swa_kvh, 2, swa_hd)
            outs["sk"].append(kv_f[:, :, 0, :, 0])
            outs["sv"].append(kv_f[:, :, 1, :, 0])
            sink = swa_sink[j].astype(F32)
            swa_args = dict(mode="pair", n_groups=swa_kvh, hps=swa_group // 2,
                            scale=float(swa_hd) ** -0.5, tq=tq)
            k_c = qkv[:tc].reshape(bc, sc, -1)
            a_c = attention(qkv, k_c, k_c, None, sink, batch=bc, n_q=sc, n_k=sc, q_row0=0,
                            k_col0=nqb, v_col0=nqb + swa_kvh, **swa_args)
            dup_cache = lambda m: jnp.tile(m[:, :, :, None, :], (1, 1, 1, 2, 1)).reshape(
                bl, past, swa_kvh * LANE).astype(BF16)
            k_l = jnp.concatenate([dup_cache(cache_swa_k[:, j]),
                                   qkv[tc:, nqb * LANE:(nqb + swa_kvh) * LANE].reshape(bl, sl, -1)], axis=1)
            v_l = jnp.concatenate([dup_cache(cache_swa_v[:, j]),
                                   qkv[tc:, (nqb + swa_kvh) * LANE:].reshape(bl, sl, -1)], axis=1)
            a_l = attention(qkv, k_l, v_l, None, sink, batch=bl, n_q=sl, n_k=past + sl, q_row0=tc,
                            k_col0=0, v_col0=0, windowed=True, n_ctx=past, **swa_args)
            w_o = swa_w_o_b

        a = matmul2(a_c, a_l, w_o, F32, tm=tm_mm, tn=512, layer=j)
        x, h = resid_norm(x, a, norm_g4, mods, i, 1, 2, i, 2, 3, rows, tm_row)

        gated = ffn_in(h, ffn_w_in, ffn_conv_w, ffn_conv_b, i, rows, tm=rows.lat_seq, tn=256, ch=ffn_ch)
        y = mm(gated, ffn_w_out_b, F32, tn=256, layer=i, single_x=True)
        if i + 1 < depth:
            x, h = resid_norm(x, y, norm_g4, mods, i, 3, 5, i + 1, 0, 0, rows, tm_row)
        else:
            x = resid(x, y, norm_g4, mods, i, 3, 5, rows, tm_row)

    stack = lambda xs: jnp.stack(xs, axis=1)
    return (x[:tc].reshape(bc, sc, d), x[tc:].reshape(bl, sl, d),
            stack(outs["ckv"]), stack(outs["kpe"]), stack(outs["gk"]), stack(outs["gv"]),
            stack(outs["sk"]), stack(outs["sv"]))


def _row_tile(m):
    for tm in (1024, 512, 256, 128, 64, 32, 16, 8):
        if m % tm == 0:
            return tm
    return m
```

```python
import functools

import jax
import jax.numpy as jnp
from jax import lax
from jax.experimental import pallas as pl
from jax.experimental.pallas import tpu as pltpu

F32 = jnp.float32
BF16 = jnp.bfloat16

EPS = 1e-6
NEG_INF = -1e30
LOG2E = 1.4426950408889634
ROPE_BASE = 10000.0
GRID_W = 64
WINDOW = 128
LANE = 128
MOD_ROWS = 16
VMEM_LIMIT = 56 * 1024 * 1024


def _cp(*sem):
    return pltpu.CompilerParams(dimension_semantics=sem, vmem_limit_bytes=VMEM_LIMIT)


def _rms(x, g):
    ms = jnp.mean(x * x, axis=-1, keepdims=True)
    return x * lax.rsqrt(ms + EPS) * g


def _adaln_kernel(c_ref, w_ref, b_ref, o_ref):
    c = c_ref[...]
    s = c / (1.0 + jnp.exp(-c))
    o_ref[...] = jnp.dot(s.astype(BF16), w_ref[...].astype(BF16),
                         preferred_element_type=F32) + b_ref[...]


def adaln(cvec, mod_w, mod_b):
    n_layers, d, n = mod_w.shape
    tn = 512 if n % 512 == 0 else n
    return pl.pallas_call(
        _adaln_kernel,
        grid=(n_layers, n // tn),
        in_specs=[pl.BlockSpec((MOD_ROWS, d), lambda l, j: (0, 0)),
                  pl.BlockSpec((None, d, tn), lambda l, j: (l, 0, j)),
                  pl.BlockSpec((None, 1, tn), lambda l, j: (l, 0, j))],
        out_specs=pl.BlockSpec((None, MOD_ROWS, tn), lambda l, j: (l, 0, j)),
        out_shape=jax.ShapeDtypeStruct((n_layers, MOD_ROWS, n), F32),
        compiler_params=_cp("parallel", "parallel"),
        name="adaln",
    )(cvec, mod_w, mod_b.reshape(n_layers, 1, n))


class Rows:
    def __init__(self, n_ctx_rows, ctx_seq, n_lat_rows, lat_seq):
        self.tc, self.ctx_seq, self.tl, self.lat_seq = n_ctx_rows, ctx_seq, n_lat_rows, lat_seq
        self.t = n_ctx_rows + n_lat_rows

    def mod_row(self, i, tm):
        nctx = self.tc // tm
        per = self.lat_seq // tm
        return jnp.where(i < nctx, 0, 1 + (i - nctx) // per)


def _mod_spec(rows, tm, layer, comp, d):
    base = layer * MOD_ROWS * 6 + comp
    return pl.BlockSpec((None, 1, d), lambda i: (base + rows.mod_row(i, tm) * 6, 0, 0))


def _gain_spec(layer, which, d):
    return pl.BlockSpec((None, None, 1, d), lambda i: (layer, which, 0, 0))


class Stream:
    def __init__(self, *parts):
        self.parts = parts

    def specs(self, rows, tm, d):
        if len(self.parts) == 1:
            return [pl.BlockSpec((tm, d), lambda i: (i, 0))]
        na = rows.tc // tm
        return [pl.BlockSpec((tm, d), lambda i: (jnp.minimum(i, na - 1), 0)),
                pl.BlockSpec((tm, d), lambda i: (jnp.maximum(i - na, 0), 0))]


def _read_stream(refs, n_ctx_blocks):
    if len(refs) == 1:
        return refs[0][...]
    return jnp.where(pl.program_id(0) < n_ctx_blocks, refs[0][...], refs[1][...])


def _normmod_kernel(*refs, nx, n_ctx_blocks):
    g_ref, sh_ref, sc_ref, o_ref = refs[nx:]
    y = _rms(_read_stream(refs[:nx], n_ctx_blocks), g_ref[...])
    o_ref[...] = (y * (1.0 + sc_ref[...]) + sh_ref[...]).astype(o_ref.dtype)


def normmod(x, norm_g4, mods, layer, which, comp, rows, tm):
    d = norm_g4.shape[-1]
    nx = len(x.parts)
    return pl.pallas_call(
        functools.partial(_normmod_kernel, nx=nx, n_ctx_blocks=rows.tc // tm),
        grid=(rows.t // tm,),
        in_specs=x.specs(rows, tm, d) + [
            _gain_spec(layer, which, d),
            _mod_spec(rows, tm, layer, comp, d), _mod_spec(rows, tm, layer, comp + 1, d)],
        out_specs=pl.BlockSpec((tm, d), lambda i: (i, 0)),
        out_shape=jax.ShapeDtypeStruct((rows.t, d), BF16),
        compiler_params=_cp("parallel"),
        name="normmod",
    )(*x.parts, norm_g4, mods, mods)


def _resid_kernel(x_ref, a_ref, g_ref, gate_ref, oc_ref, ol_ref, *, n_ctx_blocks):
    x = x_ref[...] + gate_ref[...] * _rms(a_ref[...].astype(F32), g_ref[...])

    @pl.when(pl.program_id(0) < n_ctx_blocks)
    def _():
        oc_ref[...] = x

    @pl.when(pl.program_id(0) >= n_ctx_blocks)
    def _():
        ol_ref[...] = x


def resid_split(x, a, norm_g4, mods, layer, which, comp, rows, tm):
    t, d = x.shape
    na = rows.tc // tm
    row = pl.BlockSpec((tm, d), lambda i: (i, 0))
    return pl.pallas_call(
        functools.partial(_resid_kernel, n_ctx_blocks=na),
        grid=(t // tm,),
        in_specs=[row, row, _gain_spec(layer, which, d), _mod_spec(rows, tm, layer, comp, d)],
        out_specs=[pl.BlockSpec((tm, d), lambda i: (jnp.minimum(i, na - 1), 0)),
                   pl.BlockSpec((tm, d), lambda i: (jnp.maximum(i - na, 0), 0))],
        out_shape=[jax.ShapeDtypeStruct((rows.tc, d), F32), jax.ShapeDtypeStruct((rows.tl, d), F32)],
        compiler_params=_cp("arbitrary"),
        name="resid",
    )(x, a, norm_g4, mods)


def _resid_norm_kernel(*refs, nx, n_ctx_blocks):
    a_ref, g_ref, gate_ref, g2_ref, sh_ref, sc_ref, o_ref, h_ref = refs[nx:]
    x = _read_stream(refs[:nx], n_ctx_blocks)
    x = x + gate_ref[...] * _rms(a_ref[...].astype(F32), g_ref[...])
    o_ref[...] = x
    h_ref[...] = (_rms(x, g2_ref[...]) * (1.0 + sc_ref[...]) + sh_ref[...]).astype(h_ref.dtype)


def resid_norm(x, a, norm_g4, mods, layer, which, comp, nxt_layer, nxt_which, nxt_comp, rows, tm):
    t, d = a.shape
    nx = len(x.parts)
    row = pl.BlockSpec((tm, d), lambda i: (i, 0))
    return pl.pallas_call(
        functools.partial(_resid_norm_kernel, nx=nx, n_ctx_blocks=rows.tc // tm),
        grid=(t // tm,),
        in_specs=x.specs(rows, tm, d) + [
            row, _gain_spec(layer, which, d), _mod_spec(rows, tm, layer, comp, d),
            _gain_spec(nxt_layer, nxt_which, d),
            _mod_spec(rows, tm, nxt_layer, nxt_comp, d),
            _mod_spec(rows, tm, nxt_layer, nxt_comp + 1, d)],
        out_specs=[row, row],
        out_shape=[jax.ShapeDtypeStruct((t, d), F32), jax.ShapeDtypeStruct((t, d), BF16)],
        compiler_params=_cp("parallel"),
        name="resid_norm",
    )(*x.parts, a, norm_g4, mods, norm_g4, mods, mods)


def _mm_kernel(x_ref, w_ref, o_ref):
    o_ref[...] = jnp.dot(x_ref[...], w_ref[...].astype(BF16),
                         preferred_element_type=F32).astype(o_ref.dtype)


def _mm2_kernel(xa_ref, xb_ref, w_ref, o_ref, *, na):
    @pl.when(pl.program_id(0) < na)
    def _():
        o_ref[...] = jnp.dot(xa_ref[...], w_ref[...], preferred_element_type=F32).astype(o_ref.dtype)

    @pl.when(pl.program_id(0) >= na)
    def _():
        o_ref[...] = jnp.dot(xb_ref[...], w_ref[...], preferred_element_type=F32).astype(o_ref.dtype)


def _w_spec(w, layer, k, tn, **mode):
    if w.ndim == 3:
        return pl.BlockSpec((None, k, tn), lambda i, j: (layer, 0, j), **mode)
    return pl.BlockSpec((k, tn), lambda i, j: (0, j), **mode)


def matmul(x, w, out_dtype, tm, tn, layer=0, single_x=False, single_w=False):
    m, k = x.shape
    n = w.shape[-1]
    tm, tn = min(tm, m), min(tn, n)
    assert m % tm == 0 and n % tn == 0, (m, n, tm, tn)
    x_mode = dict(pipeline_mode=pl.Buffered(1)) if single_x else {}
    w_mode = dict(pipeline_mode=pl.Buffered(1)) if single_w else {}
    return pl.pallas_call(
        _mm_kernel,
        grid=(m // tm, n // tn),
        in_specs=[pl.BlockSpec((tm, k), lambda i, j: (i, 0), **x_mode), _w_spec(w, layer, k, tn, **w_mode)],
        out_specs=pl.BlockSpec((tm, tn), lambda i, j: (i, j)),
        out_shape=jax.ShapeDtypeStruct((m, n), out_dtype),
        compiler_params=_cp("parallel", "parallel"),
        name="matmul",
    )(x, w)


def matmul2(xa, xb, w, out_dtype, tm, tn, layer=0):
    ma, k = xa.shape
    mb = xb.shape[0]
    n = w.shape[-1]
    tn = min(tn, n)
    assert ma % tm == 0 and mb % tm == 0 and n % tn == 0, (ma, mb, n, tm, tn)
    na = ma // tm
    return pl.pallas_call(
        functools.partial(_mm2_kernel, na=na),
        grid=((ma + mb) // tm, n // tn),
        in_specs=[pl.BlockSpec((tm, k), lambda i, j: (jnp.minimum(i, na - 1), 0)),
                  pl.BlockSpec((tm, k), lambda i, j: (jnp.maximum(i - na, 0), 0)),
                  _w_spec(w, layer, k, tn)],
        out_specs=pl.BlockSpec((tm, tn), lambda i, j: (i, j)),
        out_shape=jax.ShapeDtypeStruct((ma + mb, n), out_dtype),
        compiler_params=_cp("parallel", "parallel"),
        name="matmul2",
    )(xa, xb, w)


def _ffn_in_kernel(h_ref, wg_ref, wv_ref, cwg_ref, cwv_ref, cbg_ref, cbv_ref, o_ref, *,
                   n_ctx_blocks, ctx_seq, lat_seq, ch):
    tm, tn = o_ref.shape
    nch = tm // ch
    seq = jnp.where(pl.program_id(0) < n_ctx_blocks, ctx_seq, lat_seq)
    row = lax.broadcasted_iota(jnp.int32, (ch, tn), 0)
    wg = wg_ref[...].astype(BF16)
    wv = wv_ref[...].astype(BF16)
    ug = [jnp.dot(h_ref[r * ch:(r + 1) * ch, :], wg, preferred_element_type=F32) for r in range(nch)]
    uv = [jnp.dot(h_ref[r * ch:(r + 1) * ch, :], wv, preferred_element_type=F32) for r in range(nch)]

    def conv(us, r, cw_ref, cb_ref):
        u = us[r]
        pos = (row + r * ch) & (seq - 1)
        prev = pltpu.roll(u, 1, axis=0)
        if r > 0:
            prev = jnp.where(row == 0, us[r - 1][ch - 1:ch, :], prev)
        prev = jnp.where(pos == 0, 0.0, prev)
        nxt = pltpu.roll(u, ch - 1, axis=0)
        if r < nch - 1:
            nxt = jnp.where(row == ch - 1, us[r + 1][0:1, :], nxt)
        nxt = jnp.where(pos == seq - 1, 0.0, nxt)
        return prev * cw_ref[0:1, :] + u * cw_ref[1:2, :] + nxt * cw_ref[2:3, :] + cb_ref[...]

    for r in range(nch):
        gate = conv(ug, r, cwg_ref, cbg_ref)
        val = conv(uv, r, cwv_ref, cbv_ref)
        o_ref[r * ch:(r + 1) * ch, :] = (gate / (1.0 + jnp.exp(-gate)) * val).astype(o_ref.dtype)


def ffn_in(h, w_in, conv_w, conv_b, layer, rows, tm, tn, ch):
    t, d = h.shape
    f = w_in.shape[2] // 2
    assert f % tn == 0 and rows.tc % tm == 0 and tm % rows.ctx_seq == 0 and tm == rows.lat_seq
    assert rows.ctx_seq & (rows.ctx_seq - 1) == 0 and rows.lat_seq & (rows.lat_seq - 1) == 0
    assert tm % ch == 0
    nj = f // tn
    kern = functools.partial(_ffn_in_kernel, n_ctx_blocks=rows.tc // tm,
                             ctx_seq=rows.ctx_seq, lat_seq=rows.lat_seq, ch=ch)
    cb = conv_b.reshape(conv_b.shape[0], 1, 2 * f)
    return pl.pallas_call(
        kern,
        grid=(t // tm, nj),
        in_specs=[pl.BlockSpec((tm, d), lambda i, j: (i, 0), pipeline_mode=pl.Buffered(1)),
                  pl.BlockSpec((None, d, tn), lambda i, j: (layer, 0, j)),
                  pl.BlockSpec((None, d, tn), lambda i, j: (layer, 0, j + nj)),
                  pl.BlockSpec((None, 3, tn), lambda i, j: (layer, 0, j)),
                  pl.BlockSpec((None, 3, tn), lambda i, j: (layer, 0, j + nj)),
                  pl.BlockSpec((None, 1, tn), lambda i, j: (layer, 0, j)),
                  pl.BlockSpec((None, 1, tn), lambda i, j: (layer, 0, j + nj))],
        out_specs=pl.BlockSpec((tm, tn), lambda i, j: (i, j)),
        out_shape=jax.ShapeDtypeStruct((t, f), BF16),
        compiler_params=_cp("parallel", "parallel"),
        name="ffn_in",
    )(h, w_in, w_in, conv_w, conv_w, cb, cb)


def rope_tables(n_lat, head_dim, tm):
    d2 = head_dim // 2
    half = d2 // 2
    lane = jnp.arange(LANE)
    freqs = ROPE_BASE ** (-jnp.arange(half, dtype=F32) / half)
    f = freqs[lane % half]
    n = jnp.arange(n_lat)
    pos = jnp.where(((lane // d2) % 2 == 0)[None, :], (n // GRID_W)[:, None], (n % GRID_W)[:, None])
    ang = pos.astype(F32) * f[None, :]
    sign = jnp.where((lane % d2) < half, -1.0, 1.0).astype(F32)
    cos = jnp.concatenate([jnp.ones((tm, LANE), F32), jnp.cos(ang)], axis=0)
    sin = jnp.concatenate([jnp.zeros((tm, LANE), F32), jnp.sin(ang) * sign[None, :]], axis=0)
    return cos, sin, half


def _rope(x, cos, sin, half):
    lane = lax.broadcasted_iota(jnp.int32, x.shape, 1)
    partner = jnp.where((lane & (2 * half - 1)) < half, pltpu.roll(x, LANE - half, axis=1),
                        pltpu.roll(x, half, axis=1))
    return x * cos + partner * sin


def _table_spec(rows, tm):
    nctx = rows.tc // tm
    per = rows.lat_seq // tm
    return lambda i: (jnp.where(i < nctx, 0, 1 + (i - nctx) % per), 0)


def _head_rms(x, g):
    x2 = x * x
    hi = x2.astype(BF16)
    lo = (x2 - hi.astype(F32)).astype(BF16)
    ones = jnp.ones((LANE, LANE), BF16)
    ss = jnp.dot(hi, ones, preferred_element_type=F32) + jnp.dot(lo, ones, preferred_element_type=F32)
    return x * lax.rsqrt(ss * (1.0 / LANE) + EPS) * g


def _prep_kernel(x_ref, cos_ref, sin_ref, g_ref, o_ref, *, plan, half):
    cos, sin = cos_ref[...], sin_ref[...]
    for c, (kind, gi) in enumerate(plan):
        x = x_ref[:, c * LANE:(c + 1) * LANE]
        if "norm" in kind:
            x = _head_rms(x, g_ref[gi:gi + 1, :])
        if "rope" in kind:
            x = _rope(x, cos, sin, half)
        o_ref[:, c * LANE:(c + 1) * LANE] = x.astype(o_ref.dtype)


def prep(x, plan, tables, gains, rows, tm, out_dtype, col_block=0, row_blocks=None):
    cos, sin, half = tables
    width = LANE * len(plan)
    nblk = rows.t // tm if row_blocks is None else row_blocks
    tspec = _table_spec(rows, tm)
    return pl.pallas_call(
        functools.partial(_prep_kernel, plan=tuple(plan), half=half),
        grid=(nblk,),
        in_specs=[pl.BlockSpec((tm, width), lambda i: (i, col_block)),
                  pl.BlockSpec((tm, LANE), tspec),
                  pl.BlockSpec((tm, LANE), tspec),
                  pl.BlockSpec(gains.shape, lambda i: (0, 0))],
        out_specs=pl.BlockSpec((tm, width), lambda i: (i, 0)),
        out_shape=jax.ShapeDtypeStruct((nblk * tm, width), out_dtype),
        compiler_params=_cp("parallel"),
        name="prep",
    )(x, cos, sin, gains)


def _mla_prep_kernel(x_ref, cos_ref, sin_ref, gq_ref, gkv_ref,
                     cq_ref, ckvf_ref, ckvb_ref, kpef_ref, kpeb_ref, *, ql, kvl, half):
    cq_ref[...] = _rms(x_ref[:, :ql], gq_ref[...]).astype(BF16)
    ckv = _rms(x_ref[:, ql:ql + kvl], gkv_ref[...])
    ckvf_ref[...] = ckv
    ckvb_ref[...] = ckv.astype(BF16)
    kpe = x_ref[:, ql + kvl:ql + kvl + LANE]
    kpef_ref[...] = kpe
    kpeb_ref[...] = _rope(kpe, cos_ref[...], sin_ref[...], half).astype(BF16)


def mla_prep(x, gq, gkv, tables, rows, tm):
    cos, sin, half = tables
    ql, kvl = gq.shape[0], gkv.shape[0]
    t, width = x.shape
    tspec = _table_spec(rows, tm)
    row = lambda i: (i, 0)
    return pl.pallas_call(
        functools.partial(_mla_prep_kernel, ql=ql, kvl=kvl, half=half),
        grid=(t // tm,),
        in_specs=[pl.BlockSpec((tm, width), row),
                  pl.BlockSpec((tm, LANE), tspec),
                  pl.BlockSpec((tm, LANE), tspec),
                  pl.BlockSpec((1, ql), lambda i: (0, 0)),
                  pl.BlockSpec((1, kvl), lambda i: (0, 0))],
        out_specs=[pl.BlockSpec((tm, ql), row), pl.BlockSpec((tm, kvl), row),
                   pl.BlockSpec((tm, kvl), row), pl.BlockSpec((tm, LANE), row),
                   pl.BlockSpec((tm, LANE), row)],
        out_shape=[jax.ShapeDtypeStruct((t, ql), BF16), jax.ShapeDtypeStruct((t, kvl), F32),
                   jax.ShapeDtypeStruct((t, kvl), BF16), jax.ShapeDtypeStruct((t, LANE), F32),
                   jax.ShapeDtypeStruct((t, LANE), BF16)],
        compiler_params=_cp("parallel"),
        name="mla_prep",
    )(x, cos, sin, gq.reshape(1, ql), gkv.reshape(1, kvl))


def _attn_kernel(*refs, c, tq, n_ctx, n_lat, span, windowed, mode, hps):
    refs = list(refs)
    pair = mode == "pair"
    sink_ref = refs.pop(0) if pair else None
    o_ref = refs.pop()
    if mode == "mla":
        q_ref, kv_ref, kpe_ref = refs
    else:
        q_ref, k_ref, v_ref = refs
    hg = pl.program_id(1)
    qi = pl.program_id(2)

    if windowed:
        start = pl.multiple_of(jnp.clip(qi * tq - WINDOW, 0, n_lat - span), LANE)
        qpos = qi * tq + lax.broadcasted_iota(jnp.int32, (tq, span), 0)
        kpos = start + lax.broadcasted_iota(jnp.int32, (tq, span), 1)
        valid = jnp.abs(qpos - kpos) <= WINDOW
        key_rows = [(slice(0, n_ctx), None), (pl.ds(n_ctx + start, span), valid)]
    else:
        key_rows = [(slice(None), None)]

    def softmax_pv(q, ks, vs, sink):
        scores = []
        for k, (_, valid) in zip(ks, key_rows):
            s = lax.dot_general(q, k, (((1,), (1,)), ((), ())), preferred_element_type=F32)
            scores.append(s if valid is None else jnp.where(valid, s, NEG_INF))
        m = functools.reduce(jnp.maximum, [s.max(axis=-1, keepdims=True) for s in scores]) * c
        if sink is not None:
            m = jnp.maximum(m, sink * LOG2E)
        o = 0.0
        for s, v in zip(scores, vs):
            o = o + jnp.dot(jnp.exp2(s * c - m).astype(BF16), v, preferred_element_type=F32)
        return o, m

    if pair:
        lo_k = lax.broadcasted_iota(jnp.int32, (1, LANE), 1) < LANE // 2
        lo_q = lax.broadcasted_iota(jnp.int32, (tq, LANE), 1) < LANE // 2
        ks = [k_ref[r, :] for r, _ in key_rows]
        one = jnp.ones((), BF16)
        va = [jnp.where(lo_k, v_ref[r, :], one) for r, _ in key_rows]
        vb = [jnp.where(lo_k, one, v_ref[r, :]) for r, _ in key_rows]
        for t in range(hps):
            q = q_ref[:, t * LANE:(t + 1) * LANE]
            zero = jnp.zeros_like(q)
            head = (hg * hps + t) * 2
            sa, sb = sink_ref[head], sink_ref[head + 1]
            oa, ma = softmax_pv(jnp.where(lo_q, q, zero), ks, va, sa)
            ob, mb = softmax_pv(jnp.where(lo_q, zero, q), ks, vb, sb)
            da = pltpu.roll(oa, LANE // 2, axis=1) + jnp.exp2(sa * LOG2E - ma)
            db = pltpu.roll(ob, LANE // 2, axis=1) + jnp.exp2(sb * LOG2E - mb)
            o_ref[:, t * LANE:(t + 1) * LANE] = jnp.where(lo_q, oa / da, ob / db).astype(o_ref.dtype)
    else:
        if mode == "mla":
            kpe = kpe_ref[...]
        else:
            k_shared = k_ref[...]
            v_shared = jnp.concatenate([v_ref[...], jnp.ones(v_ref.shape, BF16)], axis=1)
        for t in range(hps):
            if mode == "mla":
                q = q_ref[:, t * 2 * LANE:(t + 1) * 2 * LANE]
                k = jnp.concatenate([kv_ref[:, t * 2 * LANE:t * 2 * LANE + LANE], kpe], axis=1)
                v = jnp.concatenate([kv_ref[:, t * 2 * LANE + LANE:(t + 1) * 2 * LANE],
                                     jnp.ones(kpe.shape, BF16)], axis=1)
            else:
                q = q_ref[:, t * LANE:(t + 1) * LANE]
                k, v = k_shared, v_shared
            o, _ = softmax_pv(q, [k], [v], None)
            o_ref[:, t * LANE:(t + 1) * LANE] = (o[:, :LANE] / o[:, LANE:]).astype(o_ref.dtype)


def attention(q, k, v, kpe, sink, *, mode, batch, n_q, n_k, q_row0, n_groups, hps, k_col0, v_col0,
              scale, tq, windowed=False, n_ctx=0):
    tq = min(tq, n_q)
    nq = n_q // tq
    span = min(tq + 2 * WINDOW, n_k - n_ctx) if windowed else 0
    kern = functools.partial(_attn_kernel, c=scale * LOG2E, tq=tq, n_ctx=n_ctx, n_lat=n_k - n_ctx,
                             span=span, windowed=windowed, mode=mode, hps=hps)
    row0 = q_row0 // tq
    qw = hps * (2 * LANE if mode == "mla" else LANE)
    in_specs = [pl.BlockSpec((tq, qw), lambda b, g, i: (row0 + b * nq + i, g))]
    if mode == "mla":
        in_specs += [pl.BlockSpec((None, n_k, qw), lambda b, g, i: (b, 0, g)),
                     pl.BlockSpec((None, n_k, LANE), lambda b, g, i: (b, 0, 0))]
        args = [q, k, kpe]
    else:
        in_specs += [pl.BlockSpec((None, n_k, LANE), lambda b, g, i: (b, 0, k_col0 + g)),
                     pl.BlockSpec((None, n_k, LANE), lambda b, g, i: (b, 0, v_col0 + g))]
        args = [q, k, v]
    if mode == "pair":
        in_specs.insert(0, pl.BlockSpec(memory_space=pltpu.SMEM))
        args.insert(0, sink)
    return pl.pallas_call(
        kern,
        grid=(batch, n_groups, nq),
        in_specs=in_specs,
        out_specs=pl.BlockSpec((tq, hps * LANE), lambda b, g, i: (b * nq + i, g)),
        out_shape=jax.ShapeDtypeStruct((batch * n_q, n_groups * hps * LANE), BF16),
        compiler_params=_cp("parallel", "parallel", "arbitrary"),
        name="attention",
    )(*args)


def kernel(x_prompt, x_sample, cache_mla_ckv, cache_mla_kpe, cache_gqa_k, cache_gqa_v, cache_swa_k,
           cache_swa_v, c, c_ctx, mod_w, mod_b, norm_g, ffn_w_in, ffn_conv_w, ffn_conv_b, ffn_w_out,
           mla_w_in, mla_q_norm_g, mla_kv_norm_g, mla_w_uq, mla_w_ukv, mla_w_o, gqa_w_qkv,
           gqa_q_norm_g, gqa_k_norm_g, gqa_w_o, swa_w_qkv, swa_sink, swa_w_o):
    bc, sc, d = x_prompt.shape
    bl, sl, _ = x_sample.shape
    depth = mod_w.shape[0]
    past = cache_mla_ckv.shape[2]
    rows = Rows(bc * sc, sc, bl * sl, sl)
    tc = rows.tc
    tm_row = min(256, tc, sl)
    tm_mm = min(1024, tc, sl)
    tm_prep = min(256, tc, sl)
    ffn_ch = min(512, sl // 2)
    tq = 256

    ql, kvl = mla_q_norm_g.shape[1], mla_kv_norm_g.shape[1]
    mla_rope = cache_mla_kpe.shape[3]
    mla_h = mla_w_o.shape[1] // LANE
    mla_qd = mla_w_uq.shape[2] // mla_h
    assert mla_qd - mla_rope == LANE and mla_w_ukv.shape[2] == mla_h * 2 * LANE
    mla_scale = float(mla_qd) ** -0.5
    mla_hps = 4 if mla_h % 4 == 0 else 1
    gqa_kvh, gqa_hd = cache_gqa_k.shape[3], cache_gqa_k.shape[4]
    gqa_h = gqa_w_o.shape[1] // gqa_hd
    gqa_group = gqa_h // gqa_kvh
    assert gqa_hd == LANE
    swa_kvh, swa_hd = cache_swa_k.shape[3], cache_swa_k.shape[4]
    swa_h = swa_sink.shape[1]
    swa_group = swa_h // swa_kvh
    assert 2 * swa_hd == LANE and swa_group % 2 == 0 and mla_rope == swa_hd

    x = Stream(x_prompt.reshape(tc, d), x_sample.reshape(bl * sl, d))
    cvec = jnp.concatenate([c_ctx[None], c, jnp.zeros((MOD_ROWS - 1 - bl, d), F32)], axis=0)
    mods = adaln(cvec, mod_w, mod_b).reshape(depth * MOD_ROWS * 6, 1, d)
    norm_g4 = norm_g.reshape(depth, 4, 1, d)

    tab64 = rope_tables(sl, swa_hd, tm_prep)
    tab128 = rope_tables(sl, LANE, tm_prep)
    no_gain = jnp.ones((1, LANE), F32)
    mla_w_o_b, gqa_w_o_b, swa_w_o_b = mla_w_o.astype(BF16), gqa_w_o.astype(BF16), swa_w_o.astype(BF16)
    gqa_w_qkv_b, mla_w_ukv_b = gqa_w_qkv.astype(BF16), mla_w_ukv.astype(BF16)

    outs = dict(ckv=[], kpe=[], gk=[], gv=[], sk=[], sv=[])
    mm = functools.partial(matmul, tm=tm_mm)

    h = normmod(x, norm_g4, mods, 0, 0, 0, rows, tm_row)
    for i in range(depth):
        kind, j = i % 3, i // 3

        if kind == 0:
            w_in = jnp.pad(mla_w_in[j], ((0, 0), (0, ql + kvl + LANE - mla_w_in.shape[2]))).astype(BF16)
            proj = mm(h, w_in, F32, tn=w_in.shape[1], single_w=True, tm=min(512, tm_mm))
            cq, ckv_f, ckv_b, kpe_f, kpe_b = mla_prep(proj, mla_q_norm_g[j], mla_kv_norm_g[j],
                                                      tab64, rows, tm_prep)
            outs["ckv"].append(ckv_f[:tc].reshape(bc, sc, kvl))
            outs["kpe"].append(kpe_f[:tc, :mla_rope].reshape(bc, sc, mla_rope))
            w_uq = jnp.pad(mla_w_uq[j].reshape(ql, mla_h, mla_qd),
                           ((0, 0), (0, 0), (0, 2 * LANE - mla_qd))).reshape(ql, mla_h * 2 * LANE)
            q_f = mm(cq, w_uq.astype(BF16), F32, tn=1024)
            q = prep(q_f, [("cast", 0), ("rope", 0)] * mla_h, tab64, no_gain, rows, tm_prep, BF16)
            kv_c = mm(ckv_b[:tc], mla_w_ukv_b, BF16, tn=1024, layer=j).reshape(bc, sc, -1)
            kpe_c = kpe_b[:tc].reshape(bc, sc, LANE)
            mla_args = dict(mode="mla", n_groups=mla_h // mla_hps, hps=mla_hps, k_col0=0, v_col0=0,
                            scale=mla_scale, tq=tq)
            a_c = attention(q, kv_c, None, kpe_c, None, batch=bc, n_q=sc, n_k=sc, q_row0=0, **mla_args)
            ckv_l = jnp.concatenate([cache_mla_ckv[:, j].astype(BF16), ckv_b[tc:].reshape(bl, sl, kvl)],
                                    axis=1)
            kpe_cache = jnp.pad(cache_mla_kpe[:, j], ((0, 0), (0, 0), (0, LANE - mla_rope))).astype(BF16)
            kpe_l = jnp.concatenate([kpe_cache, kpe_b[tc:].reshape(bl, sl, LANE)], axis=1)
            kv_l = mm(ckv_l.reshape(bl * (past + sl), kvl), mla_w_ukv_b, BF16, tn=1024, layer=j,
                      tm=_row_tile(bl * (past + sl))).reshape(bl, past + sl, -1)
            a_l = attention(q, kv_l, None, kpe_l, None, batch=bl, n_q=sl, n_k=past + sl, q_row0=tc,
                            **mla_args)
            w_o = mla_w_o_b
        elif kind == 1:
            nq, nkv = gqa_h, gqa_kvh
            qkv_f = mm(h, gqa_w_qkv_b, F32, tn=1024, layer=j)
            gains = jnp.stack([gqa_q_norm_g[j], gqa_k_norm_g[j]])
            plan = [("norm_rope", 0)] * nq + [("norm_rope", 1)] * nkv + [("cast", 0)] * nkv
            qkv = prep(qkv_f, plan, tab128, gains, rows, tm_prep, BF16)
            k_f = prep(qkv_f, [("norm", 1)] * nkv, tab128, gains, rows, tm_prep, F32,
                       col_block=nq // nkv, row_blocks=tc // tm_prep)
            outs["gk"].append(k_f.reshape(bc, sc, nkv, gqa_hd))
            outs["gv"].append(qkv_f[:tc, (nq + nkv) * LANE:].reshape(bc, sc, nkv, gqa_hd))
            gqa_args = dict(mode="gqa", n_groups=nkv, hps=gqa_group, scale=float(gqa_hd) ** -0.5, tq=tq)
            k_c = qkv[:tc].reshape(bc, sc, -1)
            a_c = attention(qkv, k_c, k_c, None, None, batch=bc, n_q=sc, n_k=sc, q_row0=0,
                            k_col0=nq, v_col0=nq + nkv, **gqa_args)
            k_l = jnp.concatenate([cache_gqa_k[:, j].reshape(bl, past, nkv * LANE).astype(BF16),
                                   qkv[tc:, nq * LANE:(nq + nkv) * LANE].reshape(bl, sl, nkv * LANE)],
                                  axis=1)
            v_l = jnp.concatenate([cache_gqa_v[:, j].reshape(bl, past, nkv * LANE).astype(BF16),
                                   qkv[tc:, (nq + nkv) * LANE:].reshape(bl, sl, nkv * LANE)], axis=1)
            a_l = attention(qkv, k_l, v_l, None, None, batch=bl, n_q=sl, n_k=past + sl, q_row0=tc,
                            k_col0=0, v_col0=0, **gqa_args)
            w_o = gqa_w_o_b
        else:
            nqb = swa_h // 2
            wq, wk, wv = jnp.split(swa_w_qkv[j], [swa_h * swa_hd, (swa_h + swa_kvh) * swa_hd], axis=1)
            dup = lambda m: jnp.tile(m.reshape(d, swa_kvh, 1, swa_hd), (1, 1, 2, 1)).reshape(d, -1)
            w_dup = jnp.concatenate([wq, dup(wk), dup(wv)], axis=1).astype(BF16)
            qkv_f = mm(h, w_dup, F32, tn=1024)
            plan = [("rope", 0)] * (nqb + swa_kvh) + [("cast", 0)] * swa_kvh
            qkv = prep(qkv_f, plan, tab64, no_gain, rows, tm_prep, BF16)
            kv_f = qkv_f[:tc, nqb * LANE:].reshape(bc, sc, 2, swa_kvh, 2, swa_hd)
            outs["sk"].append(kv_f[:, :, 0, :, 0])
            outs["sv"].append(kv_f[:, :, 1, :, 0])
            sink = swa_sink[j].astype(F32)
            swa_args = dict(mode="pair", n_groups=swa_kvh, hps=swa_group // 2,
                            scale=float(swa_hd) ** -0.5, tq=tq)
            k_c = qkv[:tc].reshape(bc, sc, -1)
            a_c = attention(qkv, k_c, k_c, None, sink, batch=bc, n_q=sc, n_k=sc, q_row0=0,
                            k_col0=nqb, v_col0=nqb + swa_kvh, **swa_args)
            dup_cache = lambda m: jnp.tile(m[:, :, :, None, :], (1, 1, 1, 2, 1)).reshape(
                bl, past, swa_kvh * LANE).astype(BF16)
            k_l = jnp.concatenate([dup_cache(cache_swa_k[:, j]),
                                   qkv[tc:, nqb * LANE:(nqb + swa_kvh) * LANE].reshape(bl, sl, -1)], axis=1)
            v_l = jnp.concatenate([dup_cache(cache_swa_v[:, j]),
                                   qkv[tc:, (nqb + swa_kvh) * LANE:].reshape(bl, sl, -1)], axis=1)
            a_l = attention(qkv, k_l, v_l, None, sink, batch=bl, n_q=sl, n_k=past + sl, q_row0=tc,
                            k_col0=0, v_col0=0, windowed=True, n_ctx=past, **swa_args)
            w_o = swa_w_o_b

        a = matmul2(a_c, a_l, w_o, BF16, tm=tm_mm, tn=512, layer=j)
        x, h = resid_norm(x, a, norm_g4, mods, i, 1, 2, i, 2, 3, rows, tm_row)
        x = Stream(x)

        gated = ffn_in(h, ffn_w_in, ffn_conv_w, ffn_conv_b, i, rows, tm=rows.lat_seq, tn=256, ch=ffn_ch)
        y = mm(gated, ffn_w_out, BF16, tn=256, layer=i, single_x=True)
        if i + 1 < depth:
            x, h = resid_norm(x, y, norm_g4, mods, i, 3, 5, i + 1, 0, 0, rows, tm_row)
            x = Stream(x)
        else:
            y_ctx, y_lat = resid_split(x.parts[0], y, norm_g4, mods, i, 3, 5, rows, tm_row)

    stack = lambda xs: jnp.stack(xs, axis=1)
    return (y_ctx.reshape(bc, sc, d), y_lat.reshape(bl, sl, d),
            stack(outs["ckv"]), stack(outs["kpe"]), stack(outs["gk"]), stack(outs["gv"]),
            stack(outs["sk"]), stack(outs["sv"]))


def _row_tile(m):
    for tm in (1024, 512, 256, 128, 64, 32, 16, 8):
        if m % tm == 0:
            return tm
    return m
```

```python
import functools

import jax
import jax.numpy as jnp
from jax import lax
from jax.experimental import pallas as pl
from jax.experimental.pallas import tpu as pltpu

F32 = jnp.float32
BF16 = jnp.bfloat16

EPS = 1e-6
NEG_INF = -1e30
LOG2E = 1.4426950408889634
ROPE_BASE = 10000.0
GRID_W = 64
WINDOW = 128
LANE = 128
MOD_ROWS = 16
VMEM_LIMIT = 56 * 1024 * 1024


def _cp(*sem):
    return pltpu.CompilerParams(dimension_semantics=sem, vmem_limit_bytes=VMEM_LIMIT)


def _rms(x, g):
    ms = jnp.mean(x * x, axis=-1, keepdims=True)
    return x * lax.rsqrt(ms + EPS) * g


def _adaln_kernel(c_ref, w_ref, b_ref, o_ref):
    c = c_ref[...]
    s = c / (1.0 + jnp.exp(-c))
    o_ref[...] = jnp.dot(s.astype(BF16), w_ref[...].astype(BF16),
                         preferred_element_type=F32) + b_ref[...]


def adaln(cvec, mod_w, mod_b):
    n_layers, d, n = mod_w.shape
    tn = 512 if n % 512 == 0 else n
    return pl.pallas_call(
        _adaln_kernel,
        grid=(n_layers, n // tn),
        in_specs=[pl.BlockSpec((MOD_ROWS, d), lambda l, j: (0, 0)),
                  pl.BlockSpec((None, d, tn), lambda l, j: (l, 0, j)),
                  pl.BlockSpec((None, 1, tn), lambda l, j: (l, 0, j))],
        out_specs=pl.BlockSpec((None, MOD_ROWS, tn), lambda l, j: (l, 0, j)),
        out_shape=jax.ShapeDtypeStruct((n_layers, MOD_ROWS, n), F32),
        compiler_params=_cp("parallel", "parallel"),
        name="adaln",
    )(cvec, mod_w, mod_b.reshape(n_layers, 1, n))


class Rows:
    def __init__(self, n_ctx_rows, ctx_seq, n_lat_rows, lat_seq):
        self.tc, self.ctx_seq, self.tl, self.lat_seq = n_ctx_rows, ctx_seq, n_lat_rows, lat_seq
        self.t = n_ctx_rows + n_lat_rows

    def mod_row(self, i, tm):
        nctx = self.tc // tm
        per = self.lat_seq // tm
        return jnp.where(i < nctx, 0, 1 + (i - nctx) // per)


def _mod_spec(rows, tm, layer, comp, d):
    base = layer * MOD_ROWS * 6 + comp
    return pl.BlockSpec((None, 1, d), lambda i: (base + rows.mod_row(i, tm) * 6, 0, 0))


def _gain_spec(layer, which, d):
    return pl.BlockSpec((None, None, 1, d), lambda i: (layer, which, 0, 0))


class Stream:
    def __init__(self, *parts):
        self.parts = parts

    def specs(self, rows, tm, d):
        if len(self.parts) == 1:
            return [pl.BlockSpec((tm, d), lambda i: (i, 0))]
        na = rows.tc // tm
        return [pl.BlockSpec((tm, d), lambda i: (jnp.minimum(i, na - 1), 0)),
                pl.BlockSpec((tm, d), lambda i: (jnp.maximum(i - na, 0), 0))]


def _read_stream(refs, n_ctx_blocks):
    if len(refs) == 1:
        return refs[0][...]
    return jnp.where(pl.program_id(0) < n_ctx_blocks, refs[0][...], refs[1][...])


def _normmod_kernel(*refs, nx, n_ctx_blocks):
    g_ref, sh_ref, sc_ref, o_ref = refs[nx:]
    y = _rms(_read_stream(refs[:nx], n_ctx_blocks), g_ref[...])
    o_ref[...] = (y * (1.0 + sc_ref[...]) + sh_ref[...]).astype(o_ref.dtype)


def normmod(x, norm_g4, mods, layer, which, comp, rows, tm):
    d = norm_g4.shape[-1]
    nx = len(x.parts)
    return pl.pallas_call(
        functools.partial(_normmod_kernel, nx=nx, n_ctx_blocks=rows.tc // tm),
        grid=(rows.t // tm,),
        in_specs=x.specs(rows, tm, d) + [
            _gain_spec(layer, which, d),
            _mod_spec(rows, tm, layer, comp, d), _mod_spec(rows, tm, layer, comp + 1, d)],
        out_specs=pl.BlockSpec((tm, d), lambda i: (i, 0)),
        out_shape=jax.ShapeDtypeStruct((rows.t, d), BF16),
        compiler_params=_cp("parallel"),
        name="normmod",
    )(*x.parts, norm_g4, mods, mods)


def _resid_kernel(x_ref, a_ref, g_ref, gate_ref, oc_ref, ol_ref, *, n_ctx_blocks):
    x = x_ref[...] + gate_ref[...] * _rms(a_ref[...].astype(F32), g_ref[...])

    @pl.when(pl.program_id(0) < n_ctx_blocks)
    def _():
        oc_ref[...] = x

    @pl.when(pl.program_id(0) >= n_ctx_blocks)
    def _():
        ol_ref[...] = x


def resid_split(x, a, norm_g4, mods, layer, which, comp, rows, tm):
    t, d = x.shape
    na = rows.tc // tm
    row = pl.BlockSpec((tm, d), lambda i: (i, 0))
    return pl.pallas_call(
        functools.partial(_resid_kernel, n_ctx_blocks=na),
        grid=(t // tm,),
        in_specs=[row, row, _gain_spec(layer, which, d), _mod_spec(rows, tm, layer, comp, d)],
        out_specs=[pl.BlockSpec((tm, d), lambda i: (jnp.minimum(i, na - 1), 0)),
                   pl.BlockSpec((tm, d), lambda i: (jnp.maximum(i - na, 0), 0))],
        out_shape=[jax.ShapeDtypeStruct((rows.tc, d), F32), jax.ShapeDtypeStruct((rows.tl, d), F32)],
        compiler_params=_cp("arbitrary"),
        name="resid",
    )(x, a, norm_g4, mods)


def _resid_norm_kernel(*refs, nx, n_ctx_blocks):
    a_ref, g_ref, gate_ref, g2_ref, sh_ref, sc_ref, o_ref, h_ref = refs[nx:]
    x = _read_stream(refs[:nx], n_ctx_blocks)
    x = x + gate_ref[...] * _rms(a_ref[...].astype(F32), g_ref[...])
    o_ref[...] = x
    h_ref[...] = (_rms(x, g2_ref[...]) * (1.0 + sc_ref[...]) + sh_ref[...]).astype(h_ref.dtype)


def resid_norm(x, a, norm_g4, mods, layer, which, comp, nxt_layer, nxt_which, nxt_comp, rows, tm):
    t, d = a.shape
    nx = len(x.parts)
    row = pl.BlockSpec((tm, d), lambda i: (i, 0))
    return pl.pallas_call(
        functools.partial(_resid_norm_kernel, nx=nx, n_ctx_blocks=rows.tc // tm),
        grid=(t // tm,),
        in_specs=x.specs(rows, tm, d) + [
            row, _gain_spec(layer, which, d), _mod_spec(rows, tm, layer, comp, d),
            _gain_spec(nxt_layer, nxt_which, d),
            _mod_spec(rows, tm, nxt_layer, nxt_comp, d),
            _mod_spec(rows, tm, nxt_layer, nxt_comp + 1, d)],
        out_specs=[row, row],
        out_shape=[jax.ShapeDtypeStruct((t, d), F32), jax.ShapeDtypeStruct((t, d), BF16)],
        compiler_params=_cp("parallel"),
        name="resid_norm",
    )(*x.parts, a, norm_g4, mods, norm_g4, mods, mods)


def _mm_kernel(x_ref, w_ref, o_ref):
    o_ref[...] = jnp.dot(x_ref[...], w_ref[...].astype(BF16),
                         preferred_element_type=F32).astype(o_ref.dtype)


def _mm2_kernel(xa_ref, xb_ref, w_ref, o_ref, *, na):
    @pl.when(pl.program_id(0) < na)
    def _():
        o_ref[...] = jnp.dot(xa_ref[...], w_ref[...], preferred_element_type=F32).astype(o_ref.dtype)

    @pl.when(pl.program_id(0) >= na)
    def _():
        o_ref[...] = jnp.dot(xb_ref[...], w_ref[...], preferred_element_type=F32).astype(o_ref.dtype)


def _w_spec(w, layer, k, tn, **mode):
    if w.ndim == 3:
        return pl.BlockSpec((None, k, tn), lambda i, j: (layer, 0, j), **mode)
    return pl.BlockSpec((k, tn), lambda i, j: (0, j), **mode)


def matmul(x, w, out_dtype, tm, tn, layer=0, single_x=False, single_w=False, rows=None, cols=None):
    m, k = x.shape
    m = m if rows is None else rows
    c0, n = (0, w.shape[-1]) if cols is None else cols
    tm, tn = min(tm, m), min(tn, n)
    assert m % tm == 0 and n % tn == 0 and c0 % tn == 0, (m, n, c0, tm, tn)
    j0 = c0 // tn
    x_mode = dict(pipeline_mode=pl.Buffered(1)) if single_x else {}
    w_mode = dict(pipeline_mode=pl.Buffered(1)) if single_w else {}
    if w.ndim == 3:
        w_spec = pl.BlockSpec((None, k, tn), lambda i, j: (layer, 0, j0 + j), **w_mode)
    else:
        w_spec = pl.BlockSpec((k, tn), lambda i, j: (0, j0 + j), **w_mode)
    return pl.pallas_call(
        _mm_kernel,
        grid=(m // tm, n // tn),
        in_specs=[pl.BlockSpec((tm, k), lambda i, j: (i, 0), **x_mode), w_spec],
        out_specs=pl.BlockSpec((tm, tn), lambda i, j: (i, j)),
        out_shape=jax.ShapeDtypeStruct((m, n), out_dtype),
        compiler_params=_cp("parallel", "parallel"),
        name="matmul",
    )(x, w)


def matmul2(xa, xb, w, out_dtype, tm, tn, layer=0):
    ma, k = xa.shape
    mb = xb.shape[0]
    n = w.shape[-1]
    tn = min(tn, n)
    assert ma % tm == 0 and mb % tm == 0 and n % tn == 0, (ma, mb, n, tm, tn)
    na = ma // tm
    return pl.pallas_call(
        functools.partial(_mm2_kernel, na=na),
        grid=((ma + mb) // tm, n // tn),
        in_specs=[pl.BlockSpec((tm, k), lambda i, j: (jnp.minimum(i, na - 1), 0)),
                  pl.BlockSpec((tm, k), lambda i, j: (jnp.maximum(i - na, 0), 0)),
                  _w_spec(w, layer, k, tn)],
        out_specs=pl.BlockSpec((tm, tn), lambda i, j: (i, j)),
        out_shape=jax.ShapeDtypeStruct((ma + mb, n), out_dtype),
        compiler_params=_cp("parallel", "parallel"),
        name="matmul2",
    )(xa, xb, w)


def _ffn_in_kernel(h_ref, wg_ref, wv_ref, cwg_ref, cwv_ref, cbg_ref, cbv_ref, o_ref, *,
                   n_ctx_blocks, ctx_seq, lat_seq, ch):
    tm, tn = o_ref.shape
    nch = tm // ch
    seq = jnp.where(pl.program_id(0) < n_ctx_blocks, ctx_seq, lat_seq)
    row = lax.broadcasted_iota(jnp.int32, (ch, tn), 0)
    wg = wg_ref[...].astype(BF16)
    wv = wv_ref[...].astype(BF16)
    ug = [jnp.dot(h_ref[r * ch:(r + 1) * ch, :], wg, preferred_element_type=F32) for r in range(nch)]
    uv = [jnp.dot(h_ref[r * ch:(r + 1) * ch, :], wv, preferred_element_type=F32) for r in range(nch)]

    def conv(us, r, cw_ref, cb_ref):
        u = us[r]
        pos = (row + r * ch) & (seq - 1)
        prev = pltpu.roll(u, 1, axis=0)
        if r > 0:
            prev = jnp.where(row == 0, us[r - 1][ch - 1:ch, :], prev)
        prev = jnp.where(pos == 0, 0.0, prev)
        nxt = pltpu.roll(u, ch - 1, axis=0)
        if r < nch - 1:
            nxt = jnp.where(row == ch - 1, us[r + 1][0:1, :], nxt)
        nxt = jnp.where(pos == seq - 1, 0.0, nxt)
        return prev * cw_ref[0:1, :] + u * cw_ref[1:2, :] + nxt * cw_ref[2:3, :] + cb_ref[...]

    for r in range(nch):
        gate = conv(ug, r, cwg_ref, cbg_ref)
        val = conv(uv, r, cwv_ref, cbv_ref)
        o_ref[r * ch:(r + 1) * ch, :] = (gate / (1.0 + jnp.exp(-gate)) * val).astype(o_ref.dtype)


def ffn_in(h, w_in, conv_w, conv_b, layer, rows, tm, tn, ch):
    t, d = h.shape
    f = w_in.shape[2] // 2
    assert f % tn == 0 and rows.tc % tm == 0 and tm % rows.ctx_seq == 0 and tm == rows.lat_seq
    assert rows.ctx_seq & (rows.ctx_seq - 1) == 0 and rows.lat_seq & (rows.lat_seq - 1) == 0
    assert tm % ch == 0
    nj = f // tn
    kern = functools.partial(_ffn_in_kernel, n_ctx_blocks=rows.tc // tm,
                             ctx_seq=rows.ctx_seq, lat_seq=rows.lat_seq, ch=ch)
    cb = conv_b.reshape(conv_b.shape[0], 1, 2 * f)
    return pl.pallas_call(
        kern,
        grid=(t // tm, nj),
        in_specs=[pl.BlockSpec((tm, d), lambda i, j: (i, 0), pipeline_mode=pl.Buffered(1)),
                  pl.BlockSpec((None, d, tn), lambda i, j: (layer, 0, j)),
                  pl.BlockSpec((None, d, tn), lambda i, j: (layer, 0, j + nj)),
                  pl.BlockSpec((None, 3, tn), lambda i, j: (layer, 0, j)),
                  pl.BlockSpec((None, 3, tn), lambda i, j: (layer, 0, j + nj)),
                  pl.BlockSpec((None, 1, tn), lambda i, j: (layer, 0, j)),
                  pl.BlockSpec((None, 1, tn), lambda i, j: (layer, 0, j + nj))],
        out_specs=pl.BlockSpec((tm, tn), lambda i, j: (i, j)),
        out_shape=jax.ShapeDtypeStruct((t, f), BF16),
        compiler_params=_cp("parallel", "parallel"),
        name="ffn_in",
    )(h, w_in, w_in, conv_w, conv_w, cb, cb)


def rope_tables(n_lat, head_dim, tm):
    d2 = head_dim // 2
    half = d2 // 2
    lane = jnp.arange(LANE)
    freqs = ROPE_BASE ** (-jnp.arange(half, dtype=F32) / half)
    f = freqs[lane % half]
    n = jnp.arange(n_lat)
    pos = jnp.where(((lane // d2) % 2 == 0)[None, :], (n // GRID_W)[:, None], (n % GRID_W)[:, None])
    ang = pos.astype(F32) * f[None, :]
    sign = jnp.where((lane % d2) < half, -1.0, 1.0).astype(F32)
    cos = jnp.concatenate([jnp.ones((tm, LANE), F32), jnp.cos(ang)], axis=0)
    sin = jnp.concatenate([jnp.zeros((tm, LANE), F32), jnp.sin(ang) * sign[None, :]], axis=0)
    return cos, sin, half


def _rope(x, cos, sin, half):
    lane = lax.broadcasted_iota(jnp.int32, x.shape, 1)
    partner = jnp.where((lane & (2 * half - 1)) < half, pltpu.roll(x, LANE - half, axis=1),
                        pltpu.roll(x, half, axis=1))
    return x * cos + partner * sin


def _table_spec(rows, tm):
    nctx = rows.tc // tm
    per = rows.lat_seq // tm
    return lambda i: (jnp.where(i < nctx, 0, 1 + (i - nctx) % per), 0)


def _head_rms(x, g):
    x2 = x * x
    hi = x2.astype(BF16)
    lo = (x2 - hi.astype(F32)).astype(BF16)
    ones = jnp.ones((LANE, LANE), BF16)
    ss = jnp.dot(hi, ones, preferred_element_type=F32) + jnp.dot(lo, ones, preferred_element_type=F32)
    return x * lax.rsqrt(ss * (1.0 / LANE) + EPS) * g


def _prep_kernel(x_ref, cos_ref, sin_ref, g_ref, o_ref, *, plan, half):
    cos, sin = cos_ref[...], sin_ref[...]
    for c, (kind, gi) in enumerate(plan):
        x = x_ref[:, c * LANE:(c + 1) * LANE]
        if "norm" in kind:
            x = _head_rms(x, g_ref[gi:gi + 1, :])
        if "rope" in kind:
            x = _rope(x, cos, sin, half)
        o_ref[:, c * LANE:(c + 1) * LANE] = x.astype(o_ref.dtype)


def prep(x, plan, tables, gains, rows, tm, out_dtype, col_block=0, row_blocks=None):
    cos, sin, half = tables
    width = LANE * len(plan)
    nblk = rows.t // tm if row_blocks is None else row_blocks
    tspec = _table_spec(rows, tm)
    return pl.pallas_call(
        functools.partial(_prep_kernel, plan=tuple(plan), half=half),
        grid=(nblk,),
        in_specs=[pl.BlockSpec((tm, width), lambda i: (i, col_block)),
                  pl.BlockSpec((tm, LANE), tspec),
                  pl.BlockSpec((tm, LANE), tspec),
                  pl.BlockSpec(gains.shape, lambda i: (0, 0))],
        out_specs=pl.BlockSpec((tm, width), lambda i: (i, 0)),
        out_shape=jax.ShapeDtypeStruct((nblk * tm, width), out_dtype),
        compiler_params=_cp("parallel"),
        name="prep",
    )(x, cos, sin, gains)


def _proj_heads_kernel(x_ref, w_ref, cos_ref, sin_ref, g_ref, o_ref, *,
                       ch, half, rope_blocks, n_norm_tiles):
    tm, tn = o_ref.shape
    w = w_ref[...]
    normed = pl.program_id(1) < n_norm_tiles
    for r in range(tm // ch):
        rs = slice(r * ch, (r + 1) * ch)
        acc = jnp.dot(x_ref[rs, :], w, preferred_element_type=F32)
        cos, sin = cos_ref[rs, :], sin_ref[rs, :]
        for c in range(tn // LANE):
            x = acc[:, c * LANE:(c + 1) * LANE]
            if n_norm_tiles:
                x = jnp.where(normed, _head_rms(x, g_ref[...]), x)
            if rope_blocks[c]:
                x = _rope(x, cos, sin, half)
            o_ref[rs, c * LANE:(c + 1) * LANE] = x.astype(o_ref.dtype)


def proj_heads(x, w, tables, gains, rows, tm, tn, ch, *, layer=0, n_norm_tiles=0, gain_switch=0,
               rope_blocks=None, n_rope_tiles=None):
    cos, sin, half = tables
    m, k = x.shape
    n = w.shape[-1]
    nt = n // tn
    assert m % tm == 0 and n % tn == 0 and tm % ch == 0
    rope_blocks = tuple(rope_blocks or (True,) * (tn // LANE))
    n_rope_tiles = nt if n_rope_tiles is None else n_rope_tiles
    nctx = rows.tc // tm
    per = rows.lat_seq // tm
    tspec = lambda i, j: (jnp.where((i < nctx) | (j >= n_rope_tiles), 0, 1 + (i - nctx) % per), 0)
    return pl.pallas_call(
        functools.partial(_proj_heads_kernel, ch=ch, half=half, rope_blocks=rope_blocks,
                          n_norm_tiles=n_norm_tiles),
        grid=(m // tm, nt),
        in_specs=[pl.BlockSpec((tm, k), lambda i, j: (i, 0)), _w_spec(w, layer, k, tn),
                  pl.BlockSpec((tm, LANE), tspec), pl.BlockSpec((tm, LANE), tspec),
                  pl.BlockSpec((None, 1, LANE), lambda i, j: (jnp.where(j < gain_switch, 0, 1), 0, 0))],
        out_specs=pl.BlockSpec((tm, tn), lambda i, j: (i, j)),
        out_shape=jax.ShapeDtypeStruct((m, n), BF16),
        compiler_params=_cp("parallel", "parallel"),
        name="proj_heads",
    )(x, w, cos, sin, gains.reshape(2, 1, LANE))


def _mla_prep_kernel(x_ref, cos_ref, sin_ref, gq_ref, gkv_ref,
                     cq_ref, ckvf_ref, ckvb_ref, kpef_ref, kpeb_ref, *, ql, kvl, half):
    cq_ref[...] = _rms(x_ref[:, :ql], gq_ref[...]).astype(BF16)
    ckv = _rms(x_ref[:, ql:ql + kvl], gkv_ref[...])
    ckvf_ref[...] = ckv
    ckvb_ref[...] = ckv.astype(BF16)
    kpe = x_ref[:, ql + kvl:ql + kvl + LANE]
    kpef_ref[...] = kpe
    kpeb_ref[...] = _rope(kpe, cos_ref[...], sin_ref[...], half).astype(BF16)


def mla_prep(x, gq, gkv, tables, rows, tm):
    cos, sin, half = tables
    ql, kvl = gq.shape[0], gkv.shape[0]
    t, width = x.shape
    tspec = _table_spec(rows, tm)
    row = lambda i: (i, 0)
    return pl.pallas_call(
        functools.partial(_mla_prep_kernel, ql=ql, kvl=kvl, half=half),
        grid=(t // tm,),
        in_specs=[pl.BlockSpec((tm, width), row),
                  pl.BlockSpec((tm, LANE), tspec),
                  pl.BlockSpec((tm, LANE), tspec),
                  pl.BlockSpec((1, ql), lambda i: (0, 0)),
                  pl.BlockSpec((1, kvl), lambda i: (0, 0))],
        out_specs=[pl.BlockSpec((tm, ql), row), pl.BlockSpec((tm, kvl), row),
                   pl.BlockSpec((tm, kvl), row), pl.BlockSpec((tm, LANE), row),
                   pl.BlockSpec((tm, LANE), row)],
        out_shape=[jax.ShapeDtypeStruct((t, ql), BF16), jax.ShapeDtypeStruct((t, kvl), F32),
                   jax.ShapeDtypeStruct((t, kvl), BF16), jax.ShapeDtypeStruct((t, LANE), F32),
                   jax.ShapeDtypeStruct((t, LANE), BF16)],
        compiler_params=_cp("parallel"),
        name="mla_prep",
    )(x, cos, sin, gq.reshape(1, ql), gkv.reshape(1, kvl))


def _attn_kernel(*refs, c, tq, n_ctx, n_lat, span, windowed, mode, hps):
    refs = list(refs)
    pair = mode == "pair"
    sink_ref = refs.pop(0) if pair else None
    o_ref = refs.pop()
    if mode == "mla":
        q_ref, kv_ref, kpe_ref = refs
    else:
        q_ref, k_ref, v_ref = refs
    hg = pl.program_id(1)
    qi = pl.program_id(2)

    if windowed:
        start = pl.multiple_of(jnp.clip(qi * tq - WINDOW, 0, n_lat - span), LANE)
        qpos = qi * tq + lax.broadcasted_iota(jnp.int32, (tq, span), 0)
        kpos = start + lax.broadcasted_iota(jnp.int32, (tq, span), 1)
        valid = jnp.abs(qpos - kpos) <= WINDOW
        key_rows = [(slice(0, n_ctx), None), (pl.ds(n_ctx + start, span), valid)]
    else:
        key_rows = [(slice(None), None)]

    def softmax_pv(q, ks, vs, sink):
        scores = []
        for k, (_, valid) in zip(ks, key_rows):
            s = lax.dot_general(q, k, (((1,), (1,)), ((), ())), preferred_element_type=F32)
            scores.append(s if valid is None else jnp.where(valid, s, NEG_INF))
        m = functools.reduce(jnp.maximum, [s.max(axis=-1, keepdims=True) for s in scores]) * c
        if sink is not None:
            m = jnp.maximum(m, sink * LOG2E)
        o = 0.0
        for s, v in zip(scores, vs):
            o = o + jnp.dot(jnp.exp2(s * c - m).astype(BF16), v, preferred_element_type=F32)
        return o, m

    if pair:
        lo_k = lax.broadcasted_iota(jnp.int32, (1, LANE), 1) < LANE // 2
        lo_q = lax.broadcasted_iota(jnp.int32, (tq, LANE), 1) < LANE // 2
        ks = [k_ref[r, :] for r, _ in key_rows]
        one = jnp.ones((), BF16)
        va = [jnp.where(lo_k, v_ref[r, :], one) for r, _ in key_rows]
        vb = [jnp.where(lo_k, one, v_ref[r, :]) for r, _ in key_rows]
        for t in range(hps):
            q = q_ref[:, t * LANE:(t + 1) * LANE]
            zero = jnp.zeros_like(q)
            head = (hg * hps + t) * 2
            sa, sb = sink_ref[head], sink_ref[head + 1]
            oa, ma = softmax_pv(jnp.where(lo_q, q, zero), ks, va, sa)
            ob, mb = softmax_pv(jnp.where(lo_q, zero, q), ks, vb, sb)
            da = pltpu.roll(oa, LANE // 2, axis=1) + jnp.exp2(sa * LOG2E - ma)
            db = pltpu.roll(ob, LANE // 2, axis=1) + jnp.exp2(sb * LOG2E - mb)
            o_ref[:, t * LANE:(t + 1) * LANE] = jnp.where(lo_q, oa / da, ob / db).astype(o_ref.dtype)
    else:
        if mode == "mla":
            kpe = kpe_ref[...]
        else:
            k_shared = k_ref[...]
            v_shared = jnp.concatenate([v_ref[...], jnp.ones(v_ref.shape, BF16)], axis=1)
        for t in range(hps):
            if mode == "mla":
                q = q_ref[:, t * 2 * LANE:(t + 1) * 2 * LANE]
                k = jnp.concatenate([kv_ref[:, t * 2 * LANE:t * 2 * LANE + LANE], kpe], axis=1)
                v = jnp.concatenate([kv_ref[:, t * 2 * LANE + LANE:(t + 1) * 2 * LANE],
                                     jnp.ones(kpe.shape, BF16)], axis=1)
            else:
                q = q_ref[:, t * LANE:(t + 1) * LANE]
                k, v = k_shared, v_shared
            o, _ = softmax_pv(q, [k], [v], None)
            o_ref[:, t * LANE:(t + 1) * LANE] = (o[:, :LANE] / o[:, LANE:]).astype(o_ref.dtype)


def attention(q, k, v, kpe, sink, *, mode, batch, n_q, n_k, q_row0, n_groups, hps, k_col0, v_col0,
              scale, tq, windowed=False, n_ctx=0):
    tq = min(tq, n_q)
    nq = n_q // tq
    span = min(tq + 2 * WINDOW, n_k - n_ctx) if windowed else 0
    kern = functools.partial(_attn_kernel, c=scale * LOG2E, tq=tq, n_ctx=n_ctx, n_lat=n_k - n_ctx,
                             span=span, windowed=windowed, mode=mode, hps=hps)
    row0 = q_row0 // tq
    qw = hps * (2 * LANE if mode == "mla" else LANE)
    in_specs = [pl.BlockSpec((tq, qw), lambda b, g, i: (row0 + b * nq + i, g))]
    if mode == "mla":
        in_specs += [pl.BlockSpec((None, n_k, qw), lambda b, g, i: (b, 0, g)),
                     pl.BlockSpec((None, n_k, LANE), lambda b, g, i: (b, 0, 0))]
        args = [q, k, kpe]
    else:
        in_specs += [pl.BlockSpec((None, n_k, LANE), lambda b, g, i: (b, 0, k_col0 + g)),
                     pl.BlockSpec((None, n_k, LANE), lambda b, g, i: (b, 0, v_col0 + g))]
        args = [q, k, v]
    if mode == "pair":
        in_specs.insert(0, pl.BlockSpec(memory_space=pltpu.SMEM))
        args.insert(0, sink)
    return pl.pallas_call(
        kern,
        grid=(batch, n_groups, nq),
        in_specs=in_specs,
        out_specs=pl.BlockSpec((tq, hps * LANE), lambda b, g, i: (b * nq + i, g)),
        out_shape=jax.ShapeDtypeStruct((batch * n_q, n_groups * hps * LANE), BF16),
        compiler_params=_cp("parallel", "parallel", "arbitrary"),
        name="attention",
    )(*args)


def kernel(x_prompt, x_sample, cache_mla_ckv, cache_mla_kpe, cache_gqa_k, cache_gqa_v, cache_swa_k,
           cache_swa_v, c, c_ctx, mod_w, mod_b, norm_g, ffn_w_in, ffn_conv_w, ffn_conv_b, ffn_w_out,
           mla_w_in, mla_q_norm_g, mla_kv_norm_g, mla_w_uq, mla_w_ukv, mla_w_o, gqa_w_qkv,
           gqa_q_norm_g, gqa_k_norm_g, gqa_w_o, swa_w_qkv, swa_sink, swa_w_o):
    bc, sc, d = x_prompt.shape
    bl, sl, _ = x_sample.shape
    depth = mod_w.shape[0]
    past = cache_mla_ckv.shape[2]
    rows = Rows(bc * sc, sc, bl * sl, sl)
    tc = rows.tc
    tm_row = min(256, tc, sl)
    tm_mm = min(1024, tc, sl)
    tm_prep = min(256, tc, sl)
    ffn_ch = min(256, sl // 2)
    tq_dense, tq_win = 512, 256

    ql, kvl = mla_q_norm_g.shape[1], mla_kv_norm_g.shape[1]
    mla_rope = cache_mla_kpe.shape[3]
    mla_h = mla_w_o.shape[1] // LANE
    mla_qd = mla_w_uq.shape[2] // mla_h
    assert mla_qd - mla_rope == LANE and mla_w_ukv.shape[2] == mla_h * 2 * LANE
    mla_scale = float(mla_qd) ** -0.5
    mla_hps = 4 if mla_h % 4 == 0 else 1
    gqa_kvh, gqa_hd = cache_gqa_k.shape[3], cache_gqa_k.shape[4]
    gqa_h = gqa_w_o.shape[1] // gqa_hd
    gqa_group = gqa_h // gqa_kvh
    assert gqa_hd == LANE
    swa_kvh, swa_hd = cache_swa_k.shape[3], cache_swa_k.shape[4]
    swa_h = swa_sink.shape[1]
    swa_group = swa_h // swa_kvh
    assert 2 * swa_hd == LANE and swa_group % 2 == 0 and mla_rope == swa_hd

    x = Stream(x_prompt.reshape(tc, d), x_sample.reshape(bl * sl, d))
    cvec = jnp.concatenate([c_ctx[None], c, jnp.zeros((MOD_ROWS - 1 - bl, d), F32)], axis=0)
    mods = adaln(cvec, mod_w, mod_b).reshape(depth * MOD_ROWS * 6, 1, d)
    norm_g4 = norm_g.reshape(depth, 4, 1, d)

    tab64_prep = rope_tables(sl, swa_hd, tm_prep)
    tab64 = rope_tables(sl, swa_hd, tm_mm)
    tab128 = rope_tables(sl, LANE, tm_mm)
    no_gain = jnp.ones((2, LANE), F32)
    head_ch = min(256, tm_mm)
    mla_w_o_b, gqa_w_o_b, swa_w_o_b = mla_w_o.astype(BF16), gqa_w_o.astype(BF16), swa_w_o.astype(BF16)
    gqa_w_qkv_b, mla_w_ukv_b = gqa_w_qkv.astype(BF16), mla_w_ukv.astype(BF16)

    outs = dict(ckv=[], kpe=[], gk=[], gv=[], sk=[], sv=[])
    mm = functools.partial(matmul, tm=tm_mm)

    h = normmod(x, norm_g4, mods, 0, 0, 0, rows, tm_row)
    for i in range(depth):
        kind, j = i % 3, i // 3

        if kind == 0:
            w_in = jnp.pad(mla_w_in[j], ((0, 0), (0, ql + kvl + LANE - mla_w_in.shape[2]))).astype(BF16)
            proj = mm(h, w_in, F32, tn=w_in.shape[1], single_w=True, tm=min(512, tm_mm))
            cq, ckv_f, ckv_b, kpe_f, kpe_b = mla_prep(proj, mla_q_norm_g[j], mla_kv_norm_g[j],
                                                      tab64_prep, rows, tm_prep)
            outs["ckv"].append(ckv_f[:tc].reshape(bc, sc, kvl))
            outs["kpe"].append(kpe_f[:tc, :mla_rope].reshape(bc, sc, mla_rope))
            w_uq = jnp.pad(mla_w_uq[j].reshape(ql, mla_h, mla_qd),
                           ((0, 0), (0, 0), (0, 2 * LANE - mla_qd))).reshape(ql, mla_h * 2 * LANE)
            tn_q = min(1024, mla_h * 2 * LANE)
            q = proj_heads(cq, w_uq.astype(BF16), tab64, no_gain, rows, tm_mm, tn_q, head_ch,
                           rope_blocks=(False, True) * (tn_q // (2 * LANE)))
            kv_c = mm(ckv_b[:tc], mla_w_ukv_b, BF16, tn=1024, layer=j).reshape(bc, sc, -1)
            kpe_c = kpe_b[:tc].reshape(bc, sc, LANE)
            mla_args = dict(mode="mla", n_groups=mla_h // mla_hps, hps=mla_hps, k_col0=0, v_col0=0,
                            scale=mla_scale, tq=tq_dense)
            a_c = attention(q, kv_c, None, kpe_c, None, batch=bc, n_q=sc, n_k=sc, q_row0=0, **mla_args)
            ckv_l = jnp.concatenate([cache_mla_ckv[:, j].astype(BF16), ckv_b[tc:].reshape(bl, sl, kvl)],
                                    axis=1)
            kpe_cache = jnp.pad(cache_mla_kpe[:, j], ((0, 0), (0, 0), (0, LANE - mla_rope))).astype(BF16)
            kpe_l = jnp.concatenate([kpe_cache, kpe_b[tc:].reshape(bl, sl, LANE)], axis=1)
            kv_l = mm(ckv_l.reshape(bl * (past + sl), kvl), mla_w_ukv_b, BF16, tn=1024, layer=j,
                      tm=_row_tile(bl * (past + sl))).reshape(bl, past + sl, -1)
            a_l = attention(q, kv_l, None, kpe_l, None, batch=bl, n_q=sl, n_k=past + sl, q_row0=tc,
                            **mla_args)
            w_o = mla_w_o_b
        elif kind == 1:
            nq, nkv = gqa_h, gqa_kvh
            gains = jnp.stack([gqa_q_norm_g[j], gqa_k_norm_g[j]])
            tn_g = min(1024, nkv * LANE)
            assert (nq * LANE) % tn_g == 0
            qk_tiles = (nq + nkv) * LANE // tn_g
            qkv = proj_heads(h, gqa_w_qkv_b, tab128, gains, rows, tm_mm, tn_g, head_ch, layer=j,
                             n_norm_tiles=qk_tiles, gain_switch=nq * LANE // tn_g, n_rope_tiles=qk_tiles)
            kv_f = mm(h, gqa_w_qkv_b, F32, tn=tn_g, layer=j, rows=tc, cols=(nq * LANE, 2 * nkv * LANE))
            k_f = prep(kv_f, [("norm", 1)] * nkv, tab128, gains, rows, tm_prep, F32,
                       row_blocks=tc // tm_prep)
            outs["gk"].append(k_f.reshape(bc, sc, nkv, gqa_hd))
            outs["gv"].append(kv_f[:, nkv * LANE:].reshape(bc, sc, nkv, gqa_hd))
            gqa_args = dict(mode="gqa", n_groups=nkv, hps=gqa_group, scale=float(gqa_hd) ** -0.5,
                            tq=tq_dense)
            k_c = qkv[:tc].reshape(bc, sc, -1)
            a_c = attention(qkv, k_c, k_c, None, None, batch=bc, n_q=sc, n_k=sc, q_row0=0,
                            k_col0=nq, v_col0=nq + nkv, **gqa_args)
            k_l = jnp.concatenate([cache_gqa_k[:, j].reshape(bl, past, nkv * LANE).astype(BF16),
                                   qkv[tc:, nq * LANE:(nq + nkv) * LANE].reshape(bl, sl, nkv * LANE)],
                                  axis=1)
            v_l = jnp.concatenate([cache_gqa_v[:, j].reshape(bl, past, nkv * LANE).astype(BF16),
                                   qkv[tc:, (nq + nkv) * LANE:].reshape(bl, sl, nkv * LANE)], axis=1)
            a_l = attention(qkv, k_l, v_l, None, None, batch=bl, n_q=sl, n_k=past + sl, q_row0=tc,
                            k_col0=0, v_col0=0, **gqa_args)
            w_o = gqa_w_o_b
        else:
            nqb = swa_h // 2
            wq, wk, wv = jnp.split(swa_w_qkv[j], [swa_h * swa_hd, (swa_h + swa_kvh) * swa_hd], axis=1)
            dup = lambda m: jnp.tile(m.reshape(d, swa_kvh, 1, swa_hd), (1, 1, 2, 1)).reshape(d, -1)
            w_dup = jnp.concatenate([wq, dup(wk), dup(wv)], axis=1).astype(BF16)
            tn_s = min(1024, swa_kvh * LANE)
            assert (nqb * LANE) % tn_s == 0
            qkv = proj_heads(h, w_dup, tab64, no_gain, rows, tm_mm, tn_s, head_ch,
                             n_rope_tiles=(nqb + swa_kvh) * LANE // tn_s)
            kv_f = mm(h, w_dup, F32, tn=tn_s, rows=tc, cols=(nqb * LANE, 2 * swa_kvh * LANE))
            kv_f = kv_f.reshape(bc, sc, 2, swa_kvh, 2, swa_hd)
            outs["sk"].append(kv_f[:, :, 0, :, 0])
            outs["sv"].append(kv_f[:, :, 1, :, 0])
            sink = swa_sink[j].astype(F32)
            swa_args = dict(mode="pair", n_groups=swa_kvh, hps=swa_group // 2,
                            scale=float(swa_hd) ** -0.5, tq=tq_win)
            k_c = qkv[:tc].reshape(bc, sc, -1)
            a_c = attention(qkv, k_c, k_c, None, sink, batch=bc, n_q=sc, n_k=sc, q_row0=0,
                            k_col0=nqb, v_col0=nqb + swa_kvh, **swa_args)
            dup_cache = lambda m: jnp.tile(m[:, :, :, None, :], (1, 1, 1, 2, 1)).reshape(
                bl, past, swa_kvh * LANE).astype(BF16)
            k_l = jnp.concatenate([dup_cache(cache_swa_k[:, j]),
                                   qkv[tc:, nqb * LANE:(nqb + swa_kvh) * LANE].reshape(bl, sl, -1)], axis=1)
            v_l = jnp.concatenate([dup_cache(cache_swa_v[:, j]),
                                   qkv[tc:, (nqb + swa_kvh) * LANE:].reshape(bl, sl, -1)], axis=1)
            a_l = attention(qkv, k_l, v_l, None, sink, batch=bl, n_q=sl, n_k=past + sl, q_row0=tc,
                            k_col0=0, v_col0=0, windowed=True, n_ctx=past, **swa_args)
            w_o = swa_w_o_b

        a = matmul2(a_c, a_l, w_o, BF16, tm=tm_mm, tn=512, layer=j)
        x, h = resid_norm(x, a, norm_g4, mods, i, 1, 2, i, 2, 3, rows, tm_row)
        x = Stream(x)

        gated = ffn_in(h, ffn_w_in, ffn_conv_w, ffn_conv_b, i, rows, tm=rows.lat_seq, tn=256, ch=ffn_ch)
        y = mm(gated, ffn_w_out, BF16, tn=256, layer=i, single_x=True)
        if i + 1 < depth:
            x, h = resid_norm(x, y, norm_g4, mods, i, 3, 5, i + 1, 0, 0, rows, tm_row)
            x = Stream(x)
        else:
            y_ctx, y_lat = resid_split(x.parts[0], y, norm_g4, mods, i, 3, 5, rows, tm_row)

    stack = lambda xs: jnp.stack(xs, axis=1)
    return (y_ctx.reshape(bc, sc, d), y_lat.reshape(bl, sl, d),
            stack(outs["ckv"]), stack(outs["kpe"]), stack(outs["gk"]), stack(outs["gv"]),
            stack(outs["sk"]), stack(outs["sv"]))


def _row_tile(m):
    for tm in (1024, 512, 256, 128, 64, 32, 16, 8):
        if m % tm == 0:
            return tm
    return m
```

```python
import functools

import jax
import jax.numpy as jnp
from jax import lax
from jax.experimental import pallas as pl
from jax.experimental.pallas import tpu as pltpu

F32 = jnp.float32
BF16 = jnp.bfloat16

EPS = 1e-6
NEG_INF = -1e30
LOG2E = 1.4426950408889634
ROPE_BASE = 10000.0
GRID_W = 64
WINDOW = 128
LANE = 128
MOD_ROWS = 16
VMEM_LIMIT = 56 * 1024 * 1024


def _cp(*sem):
    return pltpu.CompilerParams(dimension_semantics=sem, vmem_limit_bytes=VMEM_LIMIT)


def _rms(x, g):
    ms = jnp.mean(x * x, axis=-1, keepdims=True)
    return x * lax.rsqrt(ms + EPS) * g


def _adaln_kernel(c_ref, w_ref, b_ref, o_ref):
    c = c_ref[...]
    s = c / (1.0 + jnp.exp(-c))
    o_ref[...] = jnp.dot(s.astype(BF16), w_ref[...].astype(BF16),
                         preferred_element_type=F32) + b_ref[...]


def adaln(cvec, mod_w, mod_b):
    n_layers, d, n = mod_w.shape
    tn = 512 if n % 512 == 0 else n
    return pl.pallas_call(
        _adaln_kernel,
        grid=(n_layers, n // tn),
        in_specs=[pl.BlockSpec((MOD_ROWS, d), lambda l, j: (0, 0)),
                  pl.BlockSpec((None, d, tn), lambda l, j: (l, 0, j)),
                  pl.BlockSpec((None, 1, tn), lambda l, j: (l, 0, j))],
        out_specs=pl.BlockSpec((None, MOD_ROWS, tn), lambda l, j: (l, 0, j)),
        out_shape=jax.ShapeDtypeStruct((n_layers, MOD_ROWS, n), F32),
        compiler_params=_cp("parallel", "parallel"),
        name="adaln",
    )(cvec, mod_w, mod_b.reshape(n_layers, 1, n))


class Rows:
    def __init__(self, n_ctx_rows, ctx_seq, n_lat_rows, lat_seq):
        self.tc, self.ctx_seq, self.tl, self.lat_seq = n_ctx_rows, ctx_seq, n_lat_rows, lat_seq
        self.t = n_ctx_rows + n_lat_rows

    def mod_row(self, i, tm):
        nctx = self.tc // tm
        per = self.lat_seq // tm
        return jnp.where(i < nctx, 0, 1 + (i - nctx) // per)


def _mod_spec(rows, tm, layer, comp, d):
    base = layer * MOD_ROWS * 6 + comp
    return pl.BlockSpec((None, 1, d), lambda i: (base + rows.mod_row(i, tm) * 6, 0, 0))


def _gain_spec(layer, which, d):
    return pl.BlockSpec((None, None, 1, d), lambda i: (layer, which, 0, 0))


class Stream:
    def __init__(self, *parts):
        self.parts = parts

    def specs(self, rows, tm, d):
        if len(self.parts) == 1:
            return [pl.BlockSpec((tm, d), lambda i: (i, 0))]
        na = rows.tc // tm
        return [pl.BlockSpec((tm, d), lambda i: (jnp.minimum(i, na - 1), 0)),
                pl.BlockSpec((tm, d), lambda i: (jnp.maximum(i - na, 0), 0))]


def _read_stream(refs, n_ctx_blocks):
    if len(refs) == 1:
        return refs[0][...]
    return jnp.where(pl.program_id(0) < n_ctx_blocks, refs[0][...], refs[1][...])


def _normmod_kernel(*refs, nx, n_ctx_blocks):
    g_ref, sh_ref, sc_ref, o_ref = refs[nx:]
    y = _rms(_read_stream(refs[:nx], n_ctx_blocks), g_ref[...])
    o_ref[...] = (y * (1.0 + sc_ref[...]) + sh_ref[...]).astype(o_ref.dtype)


def normmod(x, norm_g4, mods, layer, which, comp, rows, tm):
    d = norm_g4.shape[-1]
    nx = len(x.parts)
    return pl.pallas_call(
        functools.partial(_normmod_kernel, nx=nx, n_ctx_blocks=rows.tc // tm),
        grid=(rows.t // tm,),
        in_specs=x.specs(rows, tm, d) + [
            _gain_spec(layer, which, d),
            _mod_spec(rows, tm, layer, comp, d), _mod_spec(rows, tm, layer, comp + 1, d)],
        out_specs=pl.BlockSpec((tm, d), lambda i: (i, 0)),
        out_shape=jax.ShapeDtypeStruct((rows.t, d), BF16),
        compiler_params=_cp("parallel"),
        name="normmod",
    )(*x.parts, norm_g4, mods, mods)


def _resid_kernel(x_ref, a_ref, g_ref, gate_ref, oc_ref, ol_ref, *, n_ctx_blocks):
    x = x_ref[...] + gate_ref[...] * _rms(a_ref[...].astype(F32), g_ref[...])

    @pl.when(pl.program_id(0) < n_ctx_blocks)
    def _():
        oc_ref[...] = x

    @pl.when(pl.program_id(0) >= n_ctx_blocks)
    def _():
        ol_ref[...] = x


def resid_split(x, a, norm_g4, mods, layer, which, comp, rows, tm):
    t, d = x.shape
    na = rows.tc // tm
    row = pl.BlockSpec((tm, d), lambda i: (i, 0))
    return pl.pallas_call(
        functools.partial(_resid_kernel, n_ctx_blocks=na),
        grid=(t // tm,),
        in_specs=[row, row, _gain_spec(layer, which, d), _mod_spec(rows, tm, layer, comp, d)],
        out_specs=[pl.BlockSpec((tm, d), lambda i: (jnp.minimum(i, na - 1), 0)),
                   pl.BlockSpec((tm, d), lambda i: (jnp.maximum(i - na, 0), 0))],
        out_shape=[jax.ShapeDtypeStruct((rows.tc, d), F32), jax.ShapeDtypeStruct((rows.tl, d), F32)],
        compiler_params=_cp("arbitrary"),
        name="resid",
    )(x, a, norm_g4, mods)


def _resid_norm_kernel(*refs, nx, n_ctx_blocks):
    a_ref, g_ref, gate_ref, g2_ref, sh_ref, sc_ref, o_ref, h_ref = refs[nx:]
    x = _read_stream(refs[:nx], n_ctx_blocks)
    x = x + gate_ref[...] * _rms(a_ref[...].astype(F32), g_ref[...])
    o_ref[...] = x
    h_ref[...] = (_rms(x, g2_ref[...]) * (1.0 + sc_ref[...]) + sh_ref[...]).astype(h_ref.dtype)


def resid_norm(x, a, norm_g4, mods, layer, which, comp, nxt_layer, nxt_which, nxt_comp, rows, tm):
    t, d = a.shape
    nx = len(x.parts)
    row = pl.BlockSpec((tm, d), lambda i: (i, 0))
    return pl.pallas_call(
        functools.partial(_resid_norm_kernel, nx=nx, n_ctx_blocks=rows.tc // tm),
        grid=(t // tm,),
        in_specs=x.specs(rows, tm, d) + [
            row, _gain_spec(layer, which, d), _mod_spec(rows, tm, layer, comp, d),
            _gain_spec(nxt_layer, nxt_which, d),
            _mod_spec(rows, tm, nxt_layer, nxt_comp, d),
            _mod_spec(rows, tm, nxt_layer, nxt_comp + 1, d)],
        out_specs=[row, row],
        out_shape=[jax.ShapeDtypeStruct((t, d), F32), jax.ShapeDtypeStruct((t, d), BF16)],
        compiler_params=_cp("parallel"),
        name="resid_norm",
    )(*x.parts, a, norm_g4, mods, norm_g4, mods, mods)


def _mm_kernel(x_ref, w_ref, o_ref):
    o_ref[...] = jnp.dot(x_ref[...], w_ref[...].astype(BF16),
                         preferred_element_type=F32).astype(o_ref.dtype)


def _mm2_kernel(xa_ref, xb_ref, w_ref, o_ref, *, na):
    @pl.when(pl.program_id(0) < na)
    def _():
        o_ref[...] = jnp.dot(xa_ref[...], w_ref[...], preferred_element_type=F32).astype(o_ref.dtype)

    @pl.when(pl.program_id(0) >= na)
    def _():
        o_ref[...] = jnp.dot(xb_ref[...], w_ref[...], preferred_element_type=F32).astype(o_ref.dtype)


def _w_spec(w, layer, k, tn, **mode):
    if w.ndim == 3:
        return pl.BlockSpec((None, k, tn), lambda i, j: (layer, 0, j), **mode)
    return pl.BlockSpec((k, tn), lambda i, j: (0, j), **mode)


def matmul(x, w, out_dtype, tm, tn, layer=0, single_x=False, single_w=False, rows=None, cols=None):
    m, k = x.shape
    m = m if rows is None else rows
    c0, n = (0, w.shape[-1]) if cols is None else cols
    tm, tn = min(tm, m), min(tn, n)
    assert m % tm == 0 and n % tn == 0 and c0 % tn == 0, (m, n, c0, tm, tn)
    j0 = c0 // tn
    x_mode = dict(pipeline_mode=pl.Buffered(1)) if single_x else {}
    w_mode = dict(pipeline_mode=pl.Buffered(1)) if single_w else {}
    if w.ndim == 3:
        w_spec = pl.BlockSpec((None, k, tn), lambda i, j: (layer, 0, j0 + j), **w_mode)
    else:
        w_spec = pl.BlockSpec((k, tn), lambda i, j: (0, j0 + j), **w_mode)
    return pl.pallas_call(
        _mm_kernel,
        grid=(m // tm, n // tn),
        in_specs=[pl.BlockSpec((tm, k), lambda i, j: (i, 0), **x_mode), w_spec],
        out_specs=pl.BlockSpec((tm, tn), lambda i, j: (i, j)),
        out_shape=jax.ShapeDtypeStruct((m, n), out_dtype),
        compiler_params=_cp("parallel", "parallel"),
        name="matmul",
    )(x, w)


def matmul2(xa, xb, w, out_dtype, tm, tn, layer=0):
    ma, k = xa.shape
    mb = xb.shape[0]
    n = w.shape[-1]
    tn = min(tn, n)
    assert ma % tm == 0 and mb % tm == 0 and n % tn == 0, (ma, mb, n, tm, tn)
    na = ma // tm
    return pl.pallas_call(
        functools.partial(_mm2_kernel, na=na),
        grid=((ma + mb) // tm, n // tn),
        in_specs=[pl.BlockSpec((tm, k), lambda i, j: (jnp.minimum(i, na - 1), 0)),
                  pl.BlockSpec((tm, k), lambda i, j: (jnp.maximum(i - na, 0), 0)),
                  _w_spec(w, layer, k, tn)],
        out_specs=pl.BlockSpec((tm, tn), lambda i, j: (i, j)),
        out_shape=jax.ShapeDtypeStruct((ma + mb, n), out_dtype),
        compiler_params=_cp("parallel", "parallel"),
        name="matmul2",
    )(xa, xb, w)


def _ffn_in_kernel(h_ref, wg_ref, wv_ref, cwg_ref, cwv_ref, cbg_ref, cbv_ref, o_ref, *,
                   n_ctx_blocks, ctx_seq, lat_seq, ch):
    tm, tn = o_ref.shape
    nch = tm // ch
    seq = jnp.where(pl.program_id(0) < n_ctx_blocks, ctx_seq, lat_seq)
    row = lax.broadcasted_iota(jnp.int32, (ch, tn), 0)
    wg = wg_ref[...].astype(BF16)
    wv = wv_ref[...].astype(BF16)
    ug = [jnp.dot(h_ref[r * ch:(r + 1) * ch, :], wg, preferred_element_type=F32) for r in range(nch)]
    uv = [jnp.dot(h_ref[r * ch:(r + 1) * ch, :], wv, preferred_element_type=F32) for r in range(nch)]

    def conv(us, r, cw_ref, cb_ref):
        u = us[r]
        pos = (row + r * ch) & (seq - 1)
        prev = pltpu.roll(u, 1, axis=0)
        if r > 0:
            prev = jnp.where(row == 0, us[r - 1][ch - 1:ch, :], prev)
        prev = jnp.where(pos == 0, 0.0, prev)
        nxt = pltpu.roll(u, ch - 1, axis=0)
        if r < nch - 1:
            nxt = jnp.where(row == ch - 1, us[r + 1][0:1, :], nxt)
        nxt = jnp.where(pos == seq - 1, 0.0, nxt)
        return prev * cw_ref[0:1, :] + u * cw_ref[1:2, :] + nxt * cw_ref[2:3, :] + cb_ref[...]

    for r in range(nch):
        gate = conv(ug, r, cwg_ref, cbg_ref)
        val = conv(uv, r, cwv_ref, cbv_ref)
        o_ref[r * ch:(r + 1) * ch, :] = (gate / (1.0 + jnp.exp(-gate)) * val).astype(o_ref.dtype)


def ffn_in(h, w_in, conv_w, conv_b, layer, rows, tm, tn, ch):
    t, d = h.shape
    f = w_in.shape[2] // 2
    assert f % tn == 0 and rows.tc % tm == 0 and tm % rows.ctx_seq == 0 and tm == rows.lat_seq
    assert rows.ctx_seq & (rows.ctx_seq - 1) == 0 and rows.lat_seq & (rows.lat_seq - 1) == 0
    assert tm % ch == 0
    nj = f // tn
    kern = functools.partial(_ffn_in_kernel, n_ctx_blocks=rows.tc // tm,
                             ctx_seq=rows.ctx_seq, lat_seq=rows.lat_seq, ch=ch)
    cb = conv_b.reshape(conv_b.shape[0], 1, 2 * f)
    return pl.pallas_call(
        kern,
        grid=(t // tm, nj),
        in_specs=[pl.BlockSpec((tm, d), lambda i, j: (i, 0), pipeline_mode=pl.Buffered(1)),
                  pl.BlockSpec((None, d, tn), lambda i, j: (layer, 0, j)),
                  pl.BlockSpec((None, d, tn), lambda i, j: (layer, 0, j + nj)),
                  pl.BlockSpec((None, 3, tn), lambda i, j: (layer, 0, j)),
                  pl.BlockSpec((None, 3, tn), lambda i, j: (layer, 0, j + nj)),
                  pl.BlockSpec((None, 1, tn), lambda i, j: (layer, 0, j)),
                  pl.BlockSpec((None, 1, tn), lambda i, j: (layer, 0, j + nj))],
        out_specs=pl.BlockSpec((tm, tn), lambda i, j: (i, j)),
        out_shape=jax.ShapeDtypeStruct((t, f), BF16),
        compiler_params=_cp("parallel", "parallel"),
        name="ffn_in",
    )(h, w_in, w_in, conv_w, conv_w, cb, cb)


def rope_tables(n_lat, head_dim, tm):
    d2 = head_dim // 2
    half = d2 // 2
    lane = jnp.arange(LANE)
    freqs = ROPE_BASE ** (-jnp.arange(half, dtype=F32) / half)
    f = freqs[lane % half]
    n = jnp.arange(n_lat)
    pos = jnp.where(((lane // d2) % 2 == 0)[None, :], (n // GRID_W)[:, None], (n % GRID_W)[:, None])
    ang = pos.astype(F32) * f[None, :]
    sign = jnp.where((lane % d2) < half, -1.0, 1.0).astype(F32)
    cos = jnp.concatenate([jnp.ones((tm, LANE), F32), jnp.cos(ang)], axis=0)
    sin = jnp.concatenate([jnp.zeros((tm, LANE), F32), jnp.sin(ang) * sign[None, :]], axis=0)
    return cos, sin, half


def _rope(x, cos, sin, half):
    lane = lax.broadcasted_iota(jnp.int32, x.shape, 1)
    partner = jnp.where((lane & (2 * half - 1)) < half, pltpu.roll(x, LANE - half, axis=1),
                        pltpu.roll(x, half, axis=1))
    return x * cos + partner * sin


def _table_spec(rows, tm):
    nctx = rows.tc // tm
    per = rows.lat_seq // tm
    return lambda i: (jnp.where(i < nctx, 0, 1 + (i - nctx) % per), 0)


def _head_rms(x, g):
    x2 = x * x
    hi = x2.astype(BF16)
    lo = (x2 - hi.astype(F32)).astype(BF16)
    ones = jnp.ones((LANE, LANE), BF16)
    ss = jnp.dot(hi, ones, preferred_element_type=F32) + jnp.dot(lo, ones, preferred_element_type=F32)
    return x * lax.rsqrt(ss * (1.0 / LANE) + EPS) * g


def _prep_kernel(x_ref, cos_ref, sin_ref, g_ref, o_ref, *, plan, half):
    cos, sin = cos_ref[...], sin_ref[...]
    for c, (kind, gi) in enumerate(plan):
        x = x_ref[:, c * LANE:(c + 1) * LANE]
        if "norm" in kind:
            x = _head_rms(x, g_ref[gi:gi + 1, :])
        if "rope" in kind:
            x = _rope(x, cos, sin, half)
        o_ref[:, c * LANE:(c + 1) * LANE] = x.astype(o_ref.dtype)


def prep(x, plan, tables, gains, rows, tm, out_dtype, col_block=0, row_blocks=None):
    cos, sin, half = tables
    width = LANE * len(plan)
    nblk = rows.t // tm if row_blocks is None else row_blocks
    tspec = _table_spec(rows, tm)
    return pl.pallas_call(
        functools.partial(_prep_kernel, plan=tuple(plan), half=half),
        grid=(nblk,),
        in_specs=[pl.BlockSpec((tm, width), lambda i: (i, col_block)),
                  pl.BlockSpec((tm, LANE), tspec),
                  pl.BlockSpec((tm, LANE), tspec),
                  pl.BlockSpec(gains.shape, lambda i: (0, 0))],
        out_specs=pl.BlockSpec((tm, width), lambda i: (i, 0)),
        out_shape=jax.ShapeDtypeStruct((nblk * tm, width), out_dtype),
        compiler_params=_cp("parallel"),
        name="prep",
    )(x, cos, sin, gains)


def _proj_heads_kernel(x_ref, w_ref, cos_ref, sin_ref, g_ref, o_ref, *,
                       ch, half, rope_blocks, n_norm_tiles):
    tm, tn = o_ref.shape
    w = w_ref[...]
    normed = pl.program_id(1) < n_norm_tiles
    for r in range(tm // ch):
        rs = slice(r * ch, (r + 1) * ch)
        acc = jnp.dot(x_ref[rs, :], w, preferred_element_type=F32)
        cos, sin = cos_ref[rs, :], sin_ref[rs, :]
        for c in range(tn // LANE):
            x = acc[:, c * LANE:(c + 1) * LANE]
            if n_norm_tiles:
                x = jnp.where(normed, _rms(x, g_ref[...]), x)
            if rope_blocks[c]:
                x = _rope(x, cos, sin, half)
            o_ref[rs, c * LANE:(c + 1) * LANE] = x.astype(o_ref.dtype)


def proj_heads(x, w, tables, gains, rows, tm, tn, ch, *, layer=0, n_norm_tiles=0, gain_switch=0,
               rope_blocks=None, n_rope_tiles=None):
    cos, sin, half = tables
    m, k = x.shape
    n = w.shape[-1]
    nt = n // tn
    assert m % tm == 0 and n % tn == 0 and tm % ch == 0
    rope_blocks = tuple(rope_blocks or (True,) * (tn // LANE))
    n_rope_tiles = nt if n_rope_tiles is None else n_rope_tiles
    nctx = rows.tc // tm
    per = rows.lat_seq // tm
    tspec = lambda i, j: (jnp.where((i < nctx) | (j >= n_rope_tiles), 0, 1 + (i - nctx) % per), 0)
    return pl.pallas_call(
        functools.partial(_proj_heads_kernel, ch=ch, half=half, rope_blocks=rope_blocks,
                          n_norm_tiles=n_norm_tiles),
        grid=(m // tm, nt),
        in_specs=[pl.BlockSpec((tm, k), lambda i, j: (i, 0)), _w_spec(w, layer, k, tn),
                  pl.BlockSpec((tm, LANE), tspec), pl.BlockSpec((tm, LANE), tspec),
                  pl.BlockSpec((None, 1, LANE), lambda i, j: (jnp.where(j < gain_switch, 0, 1), 0, 0))],
        out_specs=pl.BlockSpec((tm, tn), lambda i, j: (i, j)),
        out_shape=jax.ShapeDtypeStruct((m, n), BF16),
        compiler_params=_cp("parallel", "parallel"),
        name="proj_heads",
    )(x, w, cos, sin, gains.reshape(2, 1, LANE))


def _mla_prep_kernel(x_ref, cos_ref, sin_ref, gq_ref, gkv_ref,
                     cq_ref, ckvf_ref, ckvb_ref, kpef_ref, kpeb_ref, *, ql, kvl, half):
    cq_ref[...] = _rms(x_ref[:, :ql], gq_ref[...]).astype(BF16)
    ckv = _rms(x_ref[:, ql:ql + kvl], gkv_ref[...])
    ckvf_ref[...] = ckv
    ckvb_ref[...] = ckv.astype(BF16)
    kpe = x_ref[:, ql + kvl:ql + kvl + LANE]
    kpef_ref[...] = kpe
    kpeb_ref[...] = _rope(kpe, cos_ref[...], sin_ref[...], half).astype(BF16)


def mla_prep(x, gq, gkv, tables, rows, tm):
    cos, sin, half = tables
    ql, kvl = gq.shape[0], gkv.shape[0]
    t, width = x.shape
    tspec = _table_spec(rows, tm)
    row = lambda i: (i, 0)
    return pl.pallas_call(
        functools.partial(_mla_prep_kernel, ql=ql, kvl=kvl, half=half),
        grid=(t // tm,),
        in_specs=[pl.BlockSpec((tm, width), row),
                  pl.BlockSpec((tm, LANE), tspec),
                  pl.BlockSpec((tm, LANE), tspec),
                  pl.BlockSpec((1, ql), lambda i: (0, 0)),
                  pl.BlockSpec((1, kvl), lambda i: (0, 0))],
        out_specs=[pl.BlockSpec((tm, ql), row), pl.BlockSpec((tm, kvl), row),
                   pl.BlockSpec((tm, kvl), row), pl.BlockSpec((tm, LANE), row),
                   pl.BlockSpec((tm, LANE), row)],
        out_shape=[jax.ShapeDtypeStruct((t, ql), BF16), jax.ShapeDtypeStruct((t, kvl), F32),
                   jax.ShapeDtypeStruct((t, kvl), BF16), jax.ShapeDtypeStruct((t, LANE), F32),
                   jax.ShapeDtypeStruct((t, LANE), BF16)],
        compiler_params=_cp("parallel"),
        name="mla_prep",
    )(x, cos, sin, gq.reshape(1, ql), gkv.reshape(1, kvl))


def _attn_kernel(*refs, c, tq, n_lat, span, windowed, mode, hps):
    refs = list(refs)
    pair = mode == "pair"
    sink_ref = refs.pop(0) if pair else None
    o_ref = refs.pop()
    q_ref = refs.pop(0)
    if mode == "mla":
        kv_ref, kpe_ref = refs
    hg = pl.program_id(1)
    qi = pl.program_id(2)

    parts = []
    if mode != "mla":
        parts.append((refs[0], refs[1], slice(None), None))
        if len(refs) == 4 and windowed:
            start = pl.multiple_of(jnp.clip(qi * tq - WINDOW, 0, n_lat - span), LANE)
            qpos = qi * tq + lax.broadcasted_iota(jnp.int32, (tq, span), 0)
            kpos = start + lax.broadcasted_iota(jnp.int32, (tq, span), 1)
            parts.append((refs[2], refs[3], pl.ds(start, span), jnp.abs(qpos - kpos) <= WINDOW))
        elif len(refs) == 4:
            parts.append((refs[2], refs[3], slice(None), None))

    def softmax_pv(q, ks, vs, sink, valids=(None,)):
        scores = []
        for k, valid in zip(ks, valids):
            s = lax.dot_general(q, k, (((1,), (1,)), ((), ())), preferred_element_type=F32)
            scores.append(s if valid is None else jnp.where(valid, s, NEG_INF))
        m = functools.reduce(jnp.maximum, [s.max(axis=-1, keepdims=True) for s in scores]) * c
        if sink is not None:
            m = jnp.maximum(m, sink * LOG2E)
        o = 0.0
        for s, v in zip(scores, vs):
            o = o + jnp.dot(jnp.exp2(s * c - m).astype(BF16), v, preferred_element_type=F32)
        return o, m

    if pair:
        lo_k = lax.broadcasted_iota(jnp.int32, (1, LANE), 1) < LANE // 2
        lo_q = lax.broadcasted_iota(jnp.int32, (tq, LANE), 1) < LANE // 2
        valids = [valid for _, _, _, valid in parts]
        ks = [k_ref[r, :] for k_ref, _, r, _ in parts]
        one = jnp.ones((), BF16)
        va = [jnp.where(lo_k, v_ref[r, :], one) for _, v_ref, r, _ in parts]
        vb = [jnp.where(lo_k, one, v_ref[r, :]) for _, v_ref, r, _ in parts]
        for t in range(hps):
            q = q_ref[:, t * LANE:(t + 1) * LANE]
            zero = jnp.zeros_like(q)
            head = (hg * hps + t) * 2
            sa, sb = sink_ref[head], sink_ref[head + 1]
            oa, ma = softmax_pv(jnp.where(lo_q, q, zero), ks, va, sa, valids)
            ob, mb = softmax_pv(jnp.where(lo_q, zero, q), ks, vb, sb, valids)
            da = pltpu.roll(oa, LANE // 2, axis=1) + jnp.exp2(sa * LOG2E - ma)
            db = pltpu.roll(ob, LANE // 2, axis=1) + jnp.exp2(sb * LOG2E - mb)
            o_ref[:, t * LANE:(t + 1) * LANE] = jnp.where(lo_q, oa / da, ob / db).astype(o_ref.dtype)
    else:
        if mode == "mla":
            kpe = kpe_ref[...]
        else:
            ks = [k_ref[...] for k_ref, _, _, _ in parts]
            vs = [jnp.concatenate([v_ref[...], jnp.ones(v_ref.shape, BF16)], axis=1)
                  for _, v_ref, _, _ in parts]
        for t in range(hps):
            if mode == "mla":
                q = q_ref[:, t * 2 * LANE:(t + 1) * 2 * LANE]
                ks = [jnp.concatenate([kv_ref[:, t * 2 * LANE:t * 2 * LANE + LANE], kpe], axis=1)]
                vs = [jnp.concatenate([kv_ref[:, t * 2 * LANE + LANE:(t + 1) * 2 * LANE],
                                       jnp.ones(kpe.shape, BF16)], axis=1)]
            else:
                q = q_ref[:, t * LANE:(t + 1) * LANE]
            o, _ = softmax_pv(q, ks, vs, None, (None,) * len(ks))
            o_ref[:, t * LANE:(t + 1) * LANE] = (o[:, :LANE] / o[:, LANE:]).astype(o_ref.dtype)


def attention(q, k, v, kpe, sink, *, mode, batch, n_q, n_k, q_row0, n_groups, hps, k_col0, v_col0,
              scale, tq, windowed=False, self_kv=None):
    tq = min(tq, n_q)
    nq = n_q // tq
    span = min(tq + 2 * WINDOW, n_q) if windowed else 0
    kern = functools.partial(_attn_kernel, c=scale * LOG2E, tq=tq, n_lat=n_q,
                             span=span, windowed=windowed, mode=mode, hps=hps)
    row0 = q_row0 // tq
    qw = hps * (2 * LANE if mode == "mla" else LANE)
    in_specs = [pl.BlockSpec((tq, qw), lambda b, g, i: (row0 + b * nq + i, g))]
    if mode == "mla":
        in_specs += [pl.BlockSpec((None, n_k, qw), lambda b, g, i: (b, 0, g)),
                     pl.BlockSpec((None, n_k, LANE), lambda b, g, i: (b, 0, 0))]
        args = [q, k, kpe]
    else:
        in_specs += [pl.BlockSpec((None, n_k, LANE), lambda b, g, i: (b, 0, k_col0 + g)),
                     pl.BlockSpec((None, n_k, LANE), lambda b, g, i: (b, 0, v_col0 + g))]
        args = [q, k, v]
        if self_kv is not None:
            assert q_row0 % n_q == 0
            seq0 = q_row0 // n_q
            in_specs += [pl.BlockSpec((n_q, LANE), lambda b, g, i: (seq0 + b, self_kv[0] + g)),
                         pl.BlockSpec((n_q, LANE), lambda b, g, i: (seq0 + b, self_kv[1] + g))]
            args += [q, q]
    if mode == "pair":
        in_specs.insert(0, pl.BlockSpec(memory_space=pltpu.SMEM))
        args.insert(0, sink)
    return pl.pallas_call(
        kern,
        grid=(batch, n_groups, nq),
        in_specs=in_specs,
        out_specs=pl.BlockSpec((tq, hps * LANE), lambda b, g, i: (b * nq + i, g)),
        out_shape=jax.ShapeDtypeStruct((batch * n_q, n_groups * hps * LANE), BF16),
        compiler_params=_cp("parallel", "parallel", "arbitrary"),
        name="attention",
    )(*args)


def kernel(x_prompt, x_sample, cache_mla_ckv, cache_mla_kpe, cache_gqa_k, cache_gqa_v, cache_swa_k,
           cache_swa_v, c, c_ctx, mod_w, mod_b, norm_g, ffn_w_in, ffn_conv_w, ffn_conv_b, ffn_w_out,
           mla_w_in, mla_q_norm_g, mla_kv_norm_g, mla_w_uq, mla_w_ukv, mla_w_o, gqa_w_qkv,
           gqa_q_norm_g, gqa_k_norm_g, gqa_w_o, swa_w_qkv, swa_sink, swa_w_o):
    bc, sc, d = x_prompt.shape
    bl, sl, _ = x_sample.shape
    depth = mod_w.shape[0]
    past = cache_mla_ckv.shape[2]
    rows = Rows(bc * sc, sc, bl * sl, sl)
    tc = rows.tc
    tm_row = min(256, tc, sl)
    tm_mm = min(1024, tc, sl)
    tm_prep = min(256, tc, sl)
    ffn_ch = min(256, sl // 2)
    tq_dense, tq_win = 512, 256

    ql, kvl = mla_q_norm_g.shape[1], mla_kv_norm_g.shape[1]
    mla_rope = cache_mla_kpe.shape[3]
    mla_h = mla_w_o.shape[1] // LANE
    mla_qd = mla_w_uq.shape[2] // mla_h
    assert mla_qd - mla_rope == LANE and mla_w_ukv.shape[2] == mla_h * 2 * LANE
    mla_scale = float(mla_qd) ** -0.5
    mla_hps = 4 if mla_h % 4 == 0 else 1
    gqa_kvh, gqa_hd = cache_gqa_k.shape[3], cache_gqa_k.shape[4]
    gqa_h = gqa_w_o.shape[1] // gqa_hd
    gqa_group = gqa_h // gqa_kvh
    assert gqa_hd == LANE
    swa_kvh, swa_hd = cache_swa_k.shape[3], cache_swa_k.shape[4]
    swa_h = swa_sink.shape[1]
    swa_group = swa_h // swa_kvh
    assert 2 * swa_hd == LANE and swa_group % 2 == 0 and mla_rope == swa_hd

    x = Stream(x_prompt.reshape(tc, d), x_sample.reshape(bl * sl, d))
    cvec = jnp.concatenate([c_ctx[None], c, jnp.zeros((MOD_ROWS - 1 - bl, d), F32)], axis=0)
    mods = adaln(cvec, mod_w, mod_b).reshape(depth * MOD_ROWS * 6, 1, d)
    norm_g4 = norm_g.reshape(depth, 4, 1, d)

    tab64_prep = rope_tables(sl, swa_hd, tm_prep)
    tab64 = rope_tables(sl, swa_hd, tm_mm)
    tab128 = rope_tables(sl, LANE, tm_mm)
    no_gain = jnp.ones((2, LANE), F32)
    head_ch = min(128, tm_mm)
    mla_w_o_b, gqa_w_o_b, swa_w_o_b = mla_w_o.astype(BF16), gqa_w_o.astype(BF16), swa_w_o.astype(BF16)
    gqa_w_qkv_b, mla_w_ukv_b = gqa_w_qkv.astype(BF16), mla_w_ukv.astype(BF16)

    outs = dict(ckv=[], kpe=[], gk=[], gv=[], sk=[], sv=[])
    mm = functools.partial(matmul, tm=tm_mm)

    h = normmod(x, norm_g4, mods, 0, 0, 0, rows, tm_row)
    for i in range(depth):
        kind, j = i % 3, i // 3

        if kind == 0:
            w_in = jnp.pad(mla_w_in[j], ((0, 0), (0, ql + kvl + LANE - mla_w_in.shape[2]))).astype(BF16)
            proj = mm(h, w_in, F32, tn=w_in.shape[1], single_w=True, tm=min(512, tm_mm))
            cq, ckv_f, ckv_b, kpe_f, kpe_b = mla_prep(proj, mla_q_norm_g[j], mla_kv_norm_g[j],
                                                      tab64_prep, rows, tm_prep)
            outs["ckv"].append(ckv_f[:tc].reshape(bc, sc, kvl))
            outs["kpe"].append(kpe_f[:tc, :mla_rope].reshape(bc, sc, mla_rope))
            w_uq = jnp.pad(mla_w_uq[j].reshape(ql, mla_h, mla_qd),
                           ((0, 0), (0, 0), (0, 2 * LANE - mla_qd))).reshape(ql, mla_h * 2 * LANE)
            tn_q = min(1024, mla_h * 2 * LANE)
            q = proj_heads(cq, w_uq.astype(BF16), tab64, no_gain, rows, tm_mm, tn_q, head_ch,
                           rope_blocks=(False, True) * (tn_q // (2 * LANE)))
            kv_c = mm(ckv_b[:tc], mla_w_ukv_b, BF16, tn=1024, layer=j).reshape(bc, sc, -1)
            kpe_c = kpe_b[:tc].reshape(bc, sc, LANE)
            mla_args = dict(mode="mla", n_groups=mla_h // mla_hps, hps=mla_hps, k_col0=0, v_col0=0,
                            scale=mla_scale, tq=tq_dense)
            a_c = attention(q, kv_c, None, kpe_c, None, batch=bc, n_q=sc, n_k=sc, q_row0=0, **mla_args)
            ckv_l = jnp.concatenate([cache_mla_ckv[:, j].astype(BF16), ckv_b[tc:].reshape(bl, sl, kvl)],
                                    axis=1)
            kpe_cache = jnp.pad(cache_mla_kpe[:, j], ((0, 0), (0, 0), (0, LANE - mla_rope))).astype(BF16)
            kpe_l = jnp.concatenate([kpe_cache, kpe_b[tc:].reshape(bl, sl, LANE)], axis=1)
            kv_l = mm(ckv_l.reshape(bl * (past + sl), kvl), mla_w_ukv_b, BF16, tn=1024, layer=j,
                      tm=_row_tile(bl * (past + sl))).reshape(bl, past + sl, -1)
            a_l = attention(q, kv_l, None, kpe_l, None, batch=bl, n_q=sl, n_k=past + sl, q_row0=tc,
                            **mla_args)
            w_o = mla_w_o_b
        elif kind == 1:
            nq, nkv = gqa_h, gqa_kvh
            gains = jnp.stack([gqa_q_norm_g[j], gqa_k_norm_g[j]])
            tn_g = min(1024, nkv * LANE)
            assert (nq * LANE) % tn_g == 0
            qk_tiles = (nq + nkv) * LANE // tn_g
            qkv = proj_heads(h, gqa_w_qkv_b, tab128, gains, rows, tm_mm, tn_g, head_ch, layer=j,
                             n_norm_tiles=qk_tiles, gain_switch=nq * LANE // tn_g, n_rope_tiles=qk_tiles)
            kv_f = mm(h, gqa_w_qkv_b, F32, tn=tn_g, layer=j, rows=tc, cols=(nq * LANE, 2 * nkv * LANE))
            k_f = prep(kv_f, [("norm", 1)] * nkv, tab128, gains, rows, tm_prep, F32,
                       row_blocks=tc // tm_prep)
            outs["gk"].append(k_f.reshape(bc, sc, nkv, gqa_hd))
            outs["gv"].append(kv_f[:, nkv * LANE:].reshape(bc, sc, nkv, gqa_hd))
            gqa_args = dict(mode="gqa", n_groups=nkv, hps=gqa_group, scale=float(gqa_hd) ** -0.5,
                            tq=tq_dense)
            k_c = qkv[:tc].reshape(bc, sc, -1)
            a_c = attention(qkv, k_c, k_c, None, None, batch=bc, n_q=sc, n_k=sc, q_row0=0,
                            k_col0=nq, v_col0=nq + nkv, **gqa_args)
            k_p = cache_gqa_k[:, j].reshape(bl, past, nkv * LANE).astype(BF16)
            v_p = cache_gqa_v[:, j].reshape(bl, past, nkv * LANE).astype(BF16)
            a_l = attention(qkv, k_p, v_p, None, None, batch=bl, n_q=sl, n_k=past, q_row0=tc,
                            k_col0=0, v_col0=0, self_kv=(nq, nq + nkv), **gqa_args)
            w_o = gqa_w_o_b
        else:
            nqb = swa_h // 2
            wq, wk, wv = jnp.split(swa_w_qkv[j], [swa_h * swa_hd, (swa_h + swa_kvh) * swa_hd], axis=1)
            dup = lambda m: jnp.tile(m.reshape(d, swa_kvh, 1, swa_hd), (1, 1, 2, 1)).reshape(d, -1)
            w_dup = jnp.concatenate([wq, dup(wk), dup(wv)], axis=1).astype(BF16)
            tn_s = min(1024, swa_kvh * LANE)
            assert (nqb * LANE) % tn_s == 0
            qkv = proj_heads(h, w_dup, tab64, no_gain, rows, tm_mm, tn_s, head_ch,
                             n_rope_tiles=(nqb + swa_kvh) * LANE // tn_s)
            kv_f = mm(h, w_dup, F32, tn=tn_s, rows=tc, cols=(nqb * LANE, 2 * swa_kvh * LANE))
            kv_f = kv_f.reshape(bc, sc, 2, swa_kvh, 2, swa_hd)
            outs["sk"].append(kv_f[:, :, 0, :, 0])
            outs["sv"].append(kv_f[:, :, 1, :, 0])
            sink = swa_sink[j].astype(F32)
            swa_args = dict(mode="pair", n_groups=swa_kvh, hps=swa_group // 2,
                            scale=float(swa_hd) ** -0.5, tq=tq_win)
            k_c = qkv[:tc].reshape(bc, sc, -1)
            a_c = attention(qkv, k_c, k_c, None, sink, batch=bc, n_q=sc, n_k=sc, q_row0=0,
                            k_col0=nqb, v_col0=nqb + swa_kvh, **swa_args)
            dup_cache = lambda m: jnp.tile(m[:, :, :, None, :], (1, 1, 1, 2, 1)).reshape(
                bl, past, swa_kvh * LANE).astype(BF16)
            a_l = attention(qkv, dup_cache(cache_swa_k[:, j]), dup_cache(cache_swa_v[:, j]), None, sink,
                            batch=bl, n_q=sl, n_k=past, q_row0=tc, k_col0=0, v_col0=0, windowed=True,
                            self_kv=(nqb, nqb + swa_kvh), **swa_args)
            w_o = swa_w_o_b

        a = matmul2(a_c, a_l, w_o, BF16, tm=tm_mm, tn=512, layer=j)
        x, h = resid_norm(x, a, norm_g4, mods, i, 1, 2, i, 2, 3, rows, tm_row)
        x = Stream(x)

        gated = ffn_in(h, ffn_w_in, ffn_conv_w, ffn_conv_b, i, rows, tm=rows.lat_seq, tn=256, ch=ffn_ch)
        y = mm(gated, ffn_w_out, BF16, tn=256, layer=i, single_x=True)
        if i + 1 < depth:
            x, h = resid_norm(x, y, norm_g4, mods, i, 3, 5, i + 1, 0, 0, rows, tm_row)
            x = Stream(x)
        else:
            y_ctx, y_lat = resid_split(x.parts[0], y, norm_g4, mods, i, 3, 5, rows, tm_row)

    stack = lambda xs: jnp.stack(xs, axis=1)
    return (y_ctx.reshape(bc, sc, d), y_lat.reshape(bl, sl, d),
            stack(outs["ckv"]), stack(outs["kpe"]), stack(outs["gk"]), stack(outs["gv"]),
            stack(outs["sk"]), stack(outs["sv"]))


def _row_tile(m):
    for tm in (1024, 512, 256, 128, 64, 32, 16, 8):
        if m % tm == 0:
            return tm
    return m
```

```python
import functools

import jax
import jax.numpy as jnp
from jax import lax
from jax.experimental import pallas as pl
from jax.experimental.pallas import tpu as pltpu

F32 = jnp.float32
BF16 = jnp.bfloat16

EPS = 1e-6
NEG_INF = -1e30
LOG2E = 1.4426950408889634
ROPE_BASE = 10000.0
GRID_W = 64
WINDOW = 128
LANE = 128
MOD_ROWS = 16
VMEM_LIMIT = 56 * 1024 * 1024


def _cp(*sem):
    return pltpu.CompilerParams(dimension_semantics=sem, vmem_limit_bytes=VMEM_LIMIT)


def _rms(x, g):
    ms = jnp.mean(x * x, axis=-1, keepdims=True)
    return x * lax.rsqrt(ms + EPS) * g


def _adaln_kernel(c_ref, w_ref, b_ref, o_ref):
    c = c_ref[...]
    s = c / (1.0 + jnp.exp(-c))
    o_ref[...] = jnp.dot(s.astype(BF16), w_ref[...].astype(BF16),
                         preferred_element_type=F32) + b_ref[...]


def adaln(cvec, mod_w, mod_b):
    n_layers, d, n = mod_w.shape
    tn = 512 if n % 512 == 0 else n
    return pl.pallas_call(
        _adaln_kernel,
        grid=(n_layers, n // tn),
        in_specs=[pl.BlockSpec((MOD_ROWS, d), lambda l, j: (0, 0)),
                  pl.BlockSpec((None, d, tn), lambda l, j: (l, 0, j)),
                  pl.BlockSpec((None, 1, tn), lambda l, j: (l, 0, j))],
        out_specs=pl.BlockSpec((None, MOD_ROWS, tn), lambda l, j: (l, 0, j)),
        out_shape=jax.ShapeDtypeStruct((n_layers, MOD_ROWS, n), F32),
        compiler_params=_cp("parallel", "parallel"),
        name="adaln",
    )(cvec, mod_w, mod_b.reshape(n_layers, 1, n))


class Rows:
    def __init__(self, n_ctx_rows, ctx_seq, n_lat_rows, lat_seq):
        self.tc, self.ctx_seq, self.tl, self.lat_seq = n_ctx_rows, ctx_seq, n_lat_rows, lat_seq
        self.t = n_ctx_rows + n_lat_rows

    def mod_row(self, i, tm):
        nctx = self.tc // tm
        per = self.lat_seq // tm
        return jnp.where(i < nctx, 0, 1 + (i - nctx) // per)


def _mod_spec(rows, tm, layer, comp, d):
    base = layer * MOD_ROWS * 6 + comp
    return pl.BlockSpec((None, 1, d), lambda i: (base + rows.mod_row(i, tm) * 6, 0, 0))


def _gain_spec(layer, which, d):
    return pl.BlockSpec((None, None, 1, d), lambda i: (layer, which, 0, 0))


class Stream:
    def __init__(self, *parts):
        self.parts = parts

    def specs(self, rows, tm, d):
        if len(self.parts) == 1:
            return [pl.BlockSpec((tm, d), lambda i: (i, 0))]
        na = rows.tc // tm
        return [pl.BlockSpec((tm, d), lambda i: (jnp.minimum(i, na - 1), 0)),
                pl.BlockSpec((tm, d), lambda i: (jnp.maximum(i - na, 0), 0))]


def _read_stream(refs, n_ctx_blocks):
    if len(refs) == 1:
        return refs[0][...]
    return jnp.where(pl.program_id(0) < n_ctx_blocks, refs[0][...], refs[1][...])


def _normmod_kernel(*refs, nx, n_ctx_blocks):
    g_ref, sh_ref, sc_ref, o_ref = refs[nx:]
    y = _rms(_read_stream(refs[:nx], n_ctx_blocks), g_ref[...])
    o_ref[...] = (y * (1.0 + sc_ref[...]) + sh_ref[...]).astype(o_ref.dtype)


def normmod(x, norm_g4, mods, layer, which, comp, rows, tm):
    d = norm_g4.shape[-1]
    nx = len(x.parts)
    return pl.pallas_call(
        functools.partial(_normmod_kernel, nx=nx, n_ctx_blocks=rows.tc // tm),
        grid=(rows.t // tm,),
        in_specs=x.specs(rows, tm, d) + [
            _gain_spec(layer, which, d),
            _mod_spec(rows, tm, layer, comp, d), _mod_spec(rows, tm, layer, comp + 1, d)],
        out_specs=pl.BlockSpec((tm, d), lambda i: (i, 0)),
        out_shape=jax.ShapeDtypeStruct((rows.t, d), BF16),
        compiler_params=_cp("parallel"),
        name="normmod",
    )(*x.parts, norm_g4, mods, mods)


def _resid_kernel(x_ref, a_ref, g_ref, gate_ref, oc_ref, ol_ref, *, n_ctx_blocks):
    x = x_ref[...] + gate_ref[...] * _rms(a_ref[...].astype(F32), g_ref[...])

    @pl.when(pl.program_id(0) < n_ctx_blocks)
    def _():
        oc_ref[...] = x

    @pl.when(pl.program_id(0) >= n_ctx_blocks)
    def _():
        ol_ref[...] = x


def resid_split(x, a, norm_g4, mods, layer, which, comp, rows, tm):
    t, d = x.shape
    na = rows.tc // tm
    row = pl.BlockSpec((tm, d), lambda i: (i, 0))
    return pl.pallas_call(
        functools.partial(_resid_kernel, n_ctx_blocks=na),
        grid=(t // tm,),
        in_specs=[row, row, _gain_spec(layer, which, d), _mod_spec(rows, tm, layer, comp, d)],
        out_specs=[pl.BlockSpec((tm, d), lambda i: (jnp.minimum(i, na - 1), 0)),
                   pl.BlockSpec((tm, d), lambda i: (jnp.maximum(i - na, 0), 0))],
        out_shape=[jax.ShapeDtypeStruct((rows.tc, d), F32), jax.ShapeDtypeStruct((rows.tl, d), F32)],
        compiler_params=_cp("arbitrary"),
        name="resid",
    )(x, a, norm_g4, mods)


def _resid_norm_kernel(*refs, nx, n_ctx_blocks):
    a_ref, g_ref, gate_ref, g2_ref, sh_ref, sc_ref, o_ref, h_ref = refs[nx:]
    x = _read_stream(refs[:nx], n_ctx_blocks)
    x = x + gate_ref[...] * _rms(a_ref[...].astype(F32), g_ref[...])
    o_ref[...] = x
    h_ref[...] = (_rms(x, g2_ref[...]) * (1.0 + sc_ref[...]) + sh_ref[...]).astype(h_ref.dtype)


def resid_norm(x, a, norm_g4, mods, layer, which, comp, nxt_layer, nxt_which, nxt_comp, rows, tm):
    t, d = a.shape
    nx = len(x.parts)
    row = pl.BlockSpec((tm, d), lambda i: (i, 0))
    return pl.pallas_call(
        functools.partial(_resid_norm_kernel, nx=nx, n_ctx_blocks=rows.tc // tm),
        grid=(t // tm,),
        in_specs=x.specs(rows, tm, d) + [
            row, _gain_spec(layer, which, d), _mod_spec(rows, tm, layer, comp, d),
            _gain_spec(nxt_layer, nxt_which, d),
            _mod_spec(rows, tm, nxt_layer, nxt_comp, d),
            _mod_spec(rows, tm, nxt_layer, nxt_comp + 1, d)],
        out_specs=[row, row],
        out_shape=[jax.ShapeDtypeStruct((t, d), F32), jax.ShapeDtypeStruct((t, d), BF16)],
        compiler_params=_cp("parallel"),
        name="resid_norm",
    )(*x.parts, a, norm_g4, mods, norm_g4, mods, mods)


def _mm_kernel(x_ref, w_ref, o_ref):
    o_ref[...] = jnp.dot(x_ref[...], w_ref[...].astype(BF16),
                         preferred_element_type=F32).astype(o_ref.dtype)


def _mm2_kernel(xa_ref, xb_ref, w_ref, o_ref, *, na):
    @pl.when(pl.program_id(0) < na)
    def _():
        o_ref[...] = jnp.dot(xa_ref[...], w_ref[...], preferred_element_type=F32).astype(o_ref.dtype)

    @pl.when(pl.program_id(0) >= na)
    def _():
        o_ref[...] = jnp.dot(xb_ref[...], w_ref[...], preferred_element_type=F32).astype(o_ref.dtype)


def _w_spec(w, layer, k, tn, **mode):
    if w.ndim == 3:
        return pl.BlockSpec((None, k, tn), lambda i, j: (layer, 0, j), **mode)
    return pl.BlockSpec((k, tn), lambda i, j: (0, j), **mode)


def matmul(x, w, out_dtype, tm, tn, layer=0, single_x=False, single_w=False, rows=None, cols=None):
    m, k = x.shape
    m = m if rows is None else rows
    c0, n = (0, w.shape[-1]) if cols is None else cols
    tm, tn = min(tm, m), min(tn, n)
    assert m % tm == 0 and n % tn == 0 and c0 % tn == 0, (m, n, c0, tm, tn)
    j0 = c0 // tn
    x_mode = dict(pipeline_mode=pl.Buffered(1)) if single_x else {}
    w_mode = dict(pipeline_mode=pl.Buffered(1)) if single_w else {}
    if w.ndim == 3:
        w_spec = pl.BlockSpec((None, k, tn), lambda i, j: (layer, 0, j0 + j), **w_mode)
    else:
        w_spec = pl.BlockSpec((k, tn), lambda i, j: (0, j0 + j), **w_mode)
    return pl.pallas_call(
        _mm_kernel,
        grid=(m // tm, n // tn),
        in_specs=[pl.BlockSpec((tm, k), lambda i, j: (i, 0), **x_mode), w_spec],
        out_specs=pl.BlockSpec((tm, tn), lambda i, j: (i, j)),
        out_shape=jax.ShapeDtypeStruct((m, n), out_dtype),
        compiler_params=_cp("parallel", "parallel"),
        name="matmul",
    )(x, w)


def matmul2(xa, xb, w, out_dtype, tm, tn, layer=0):
    ma, k = xa.shape
    mb = xb.shape[0]
    n = w.shape[-1]
    tn = min(tn, n)
    assert ma % tm == 0 and mb % tm == 0 and n % tn == 0, (ma, mb, n, tm, tn)
    na = ma // tm
    return pl.pallas_call(
        functools.partial(_mm2_kernel, na=na),
        grid=((ma + mb) // tm, n // tn),
        in_specs=[pl.BlockSpec((tm, k), lambda i, j: (jnp.minimum(i, na - 1), 0)),
                  pl.BlockSpec((tm, k), lambda i, j: (jnp.maximum(i - na, 0), 0)),
                  _w_spec(w, layer, k, tn)],
        out_specs=pl.BlockSpec((tm, tn), lambda i, j: (i, j)),
        out_shape=jax.ShapeDtypeStruct((ma + mb, n), out_dtype),
        compiler_params=_cp("parallel", "parallel"),
        name="matmul2",
    )(xa, xb, w)


def _ffn_in_kernel(h_ref, wg_ref, wv_ref, cwg_ref, cwv_ref, cbg_ref, cbv_ref, o_ref, *,
                   n_ctx_blocks, ctx_seq, lat_seq, ch):
    tm, tn = o_ref.shape
    nch = tm // ch
    seq = jnp.where(pl.program_id(0) < n_ctx_blocks, ctx_seq, lat_seq)
    row = lax.broadcasted_iota(jnp.int32, (ch, tn), 0)
    wg = wg_ref[...].astype(BF16)
    wv = wv_ref[...].astype(BF16)
    ug = [jnp.dot(h_ref[r * ch:(r + 1) * ch, :], wg, preferred_element_type=F32) for r in range(nch)]
    uv = [jnp.dot(h_ref[r * ch:(r + 1) * ch, :], wv, preferred_element_type=F32) for r in range(nch)]

    def conv(us, r, cw_ref, cb_ref):
        u = us[r]
        pos = (row + r * ch) & (seq - 1)
        prev = pltpu.roll(u, 1, axis=0)
        if r > 0:
            prev = jnp.where(row == 0, us[r - 1][ch - 1:ch, :], prev)
        prev = jnp.where(pos == 0, 0.0, prev)
        nxt = pltpu.roll(u, ch - 1, axis=0)
        if r < nch - 1:
            nxt = jnp.where(row == ch - 1, us[r + 1][0:1, :], nxt)
        nxt = jnp.where(pos == seq - 1, 0.0, nxt)
        return prev * cw_ref[0:1, :] + u * cw_ref[1:2, :] + nxt * cw_ref[2:3, :] + cb_ref[...]

    for r in range(nch):
        gate = conv(ug, r, cwg_ref, cbg_ref)
        val = conv(uv, r, cwv_ref, cbv_ref)
        o_ref[r * ch:(r + 1) * ch, :] = (gate / (1.0 + jnp.exp(-gate)) * val).astype(o_ref.dtype)


def ffn_in(h, w_in, conv_w, conv_b, layer, rows, tm, tn, ch):
    t, d = h.shape
    f = w_in.shape[2] // 2
    assert f % tn == 0 and rows.tc % tm == 0 and tm % rows.ctx_seq == 0 and tm == rows.lat_seq
    assert rows.ctx_seq & (rows.ctx_seq - 1) == 0 and rows.lat_seq & (rows.lat_seq - 1) == 0
    assert tm % ch == 0
    nj = f // tn
    kern = functools.partial(_ffn_in_kernel, n_ctx_blocks=rows.tc // tm,
                             ctx_seq=rows.ctx_seq, lat_seq=rows.lat_seq, ch=ch)
    cb = conv_b.reshape(conv_b.shape[0], 1, 2 * f)
    return pl.pallas_call(
        kern,
        grid=(t // tm, nj),
        in_specs=[pl.BlockSpec((tm, d), lambda i, j: (i, 0), pipeline_mode=pl.Buffered(1)),
                  pl.BlockSpec((None, d, tn), lambda i, j: (layer, 0, j)),
                  pl.BlockSpec((None, d, tn), lambda i, j: (layer, 0, j + nj)),
                  pl.BlockSpec((None, 3, tn), lambda i, j: (layer, 0, j)),
                  pl.BlockSpec((None, 3, tn), lambda i, j: (layer, 0, j + nj)),
                  pl.BlockSpec((None, 1, tn), lambda i, j: (layer, 0, j)),
                  pl.BlockSpec((None, 1, tn), lambda i, j: (layer, 0, j + nj))],
        out_specs=pl.BlockSpec((tm, tn), lambda i, j: (i, j)),
        out_shape=jax.ShapeDtypeStruct((t, f), BF16),
        compiler_params=_cp("parallel", "parallel"),
        name="ffn_in",
    )(h, w_in, w_in, conv_w, conv_w, cb, cb)


def rope_tables(n_lat, head_dim, tm):
    d2 = head_dim // 2
    half = d2 // 2
    lane = jnp.arange(LANE)
    freqs = ROPE_BASE ** (-jnp.arange(half, dtype=F32) / half)
    f = freqs[lane % half]
    n = jnp.arange(n_lat)
    pos = jnp.where(((lane // d2) % 2 == 0)[None, :], (n // GRID_W)[:, None], (n % GRID_W)[:, None])
    ang = pos.astype(F32) * f[None, :]
    sign = jnp.where((lane % d2) < half, -1.0, 1.0).astype(F32)
    cos = jnp.concatenate([jnp.ones((tm, LANE), F32), jnp.cos(ang)], axis=0)
    sin = jnp.concatenate([jnp.zeros((tm, LANE), F32), jnp.sin(ang) * sign[None, :]], axis=0)
    return cos, sin, half


def _rope(x, cos, sin, half):
    lane = lax.broadcasted_iota(jnp.int32, x.shape, 1)
    partner = jnp.where((lane & (2 * half - 1)) < half, pltpu.roll(x, LANE - half, axis=1),
                        pltpu.roll(x, half, axis=1))
    return x * cos + partner * sin


def _table_spec(rows, tm):
    nctx = rows.tc // tm
    per = rows.lat_seq // tm
    return lambda i: (jnp.where(i < nctx, 0, 1 + (i - nctx) % per), 0)


def _head_norm_kernel(x_ref, g_ref, o_ref):
    for c in range(o_ref.shape[1] // LANE):
        cols = slice(c * LANE, (c + 1) * LANE)
        o_ref[:, cols] = _rms(x_ref[:, cols], g_ref[...])


def head_norm(x, g, width, tm):
    m = x.shape[0]
    return pl.pallas_call(
        _head_norm_kernel,
        grid=(m // tm,),
        in_specs=[pl.BlockSpec((tm, width), lambda i: (i, 0)), pl.BlockSpec((1, LANE), lambda i: (0, 0))],
        out_specs=pl.BlockSpec((tm, width), lambda i: (i, 0)),
        out_shape=jax.ShapeDtypeStruct((m, width), F32),
        compiler_params=_cp("parallel"),
        name="head_norm",
    )(x, g.reshape(1, LANE))


def _proj_heads_kernel(x_ref, w_ref, cos_ref, sin_ref, g_ref, o_ref, *,
                       ch, half, rope_blocks, n_norm_tiles):
    tm, tn = o_ref.shape
    w = w_ref[...]
    normed = pl.program_id(1) < n_norm_tiles
    for r in range(tm // ch):
        rs = slice(r * ch, (r + 1) * ch)
        acc = jnp.dot(x_ref[rs, :], w, preferred_element_type=F32)
        cos, sin = cos_ref[rs, :], sin_ref[rs, :]
        for c in range(tn // LANE):
            x = acc[:, c * LANE:(c + 1) * LANE]
            if n_norm_tiles:
                x = jnp.where(normed, _rms(x, g_ref[...]), x)
            if rope_blocks[c]:
                x = _rope(x, cos, sin, half)
            o_ref[rs, c * LANE:(c + 1) * LANE] = x.astype(o_ref.dtype)


def proj_heads(x, w, tables, gains, rows, tm, tn, ch, *, layer=0, n_norm_tiles=0, gain_switch=0,
               rope_blocks=None, n_rope_tiles=None):
    cos, sin, half = tables
    m, k = x.shape
    n = w.shape[-1]
    nt = n // tn
    assert m % tm == 0 and n % tn == 0 and tm % ch == 0
    rope_blocks = tuple(rope_blocks or (True,) * (tn // LANE))
    n_rope_tiles = nt if n_rope_tiles is None else n_rope_tiles
    nctx = rows.tc // tm
    per = rows.lat_seq // tm
    tspec = lambda i, j: (jnp.where((i < nctx) | (j >= n_rope_tiles), 0, 1 + (i - nctx) % per), 0)
    return pl.pallas_call(
        functools.partial(_proj_heads_kernel, ch=ch, half=half, rope_blocks=rope_blocks,
                          n_norm_tiles=n_norm_tiles),
        grid=(m // tm, nt),
        in_specs=[pl.BlockSpec((tm, k), lambda i, j: (i, 0)), _w_spec(w, layer, k, tn),
                  pl.BlockSpec((tm, LANE), tspec), pl.BlockSpec((tm, LANE), tspec),
                  pl.BlockSpec((None, 1, LANE), lambda i, j: (jnp.where(j < gain_switch, 0, 1), 0, 0))],
        out_specs=pl.BlockSpec((tm, tn), lambda i, j: (i, j)),
        out_shape=jax.ShapeDtypeStruct((m, n), BF16),
        compiler_params=_cp("parallel", "parallel"),
        name="proj_heads",
    )(x, w, cos, sin, gains.reshape(2, 1, LANE))


def _mla_prep_kernel(x_ref, cos_ref, sin_ref, gq_ref, gkv_ref,
                     cq_ref, ckvf_ref, ckvb_ref, kpef_ref, kpeb_ref, *, ql, kvl, half):
    cq_ref[...] = _rms(x_ref[:, :ql], gq_ref[...]).astype(BF16)
    ckv = _rms(x_ref[:, ql:ql + kvl], gkv_ref[...])
    ckvf_ref[...] = ckv
    ckvb_ref[...] = ckv.astype(BF16)
    kpe = x_ref[:, ql + kvl:ql + kvl + LANE]
    kpef_ref[...] = kpe
    kpeb_ref[...] = _rope(kpe, cos_ref[...], sin_ref[...], half).astype(BF16)


def mla_prep(x, gq, gkv, tables, rows, tm):
    cos, sin, half = tables
    ql, kvl = gq.shape[0], gkv.shape[0]
    t, width = x.shape
    tspec = _table_spec(rows, tm)
    row = lambda i: (i, 0)
    return pl.pallas_call(
        functools.partial(_mla_prep_kernel, ql=ql, kvl=kvl, half=half),
        grid=(t // tm,),
        in_specs=[pl.BlockSpec((tm, width), row),
                  pl.BlockSpec((tm, LANE), tspec),
                  pl.BlockSpec((tm, LANE), tspec),
                  pl.BlockSpec((1, ql), lambda i: (0, 0)),
                  pl.BlockSpec((1, kvl), lambda i: (0, 0))],
        out_specs=[pl.BlockSpec((tm, ql), row), pl.BlockSpec((tm, kvl), row),
                   pl.BlockSpec((tm, kvl), row), pl.BlockSpec((tm, LANE), row),
                   pl.BlockSpec((tm, LANE), row)],
        out_shape=[jax.ShapeDtypeStruct((t, ql), BF16), jax.ShapeDtypeStruct((t, kvl), F32),
                   jax.ShapeDtypeStruct((t, kvl), BF16), jax.ShapeDtypeStruct((t, LANE), F32),
                   jax.ShapeDtypeStruct((t, LANE), BF16)],
        compiler_params=_cp("parallel"),
        name="mla_prep",
    )(x, cos, sin, gq.reshape(1, ql), gkv.reshape(1, kvl))


def _attn_kernel(*refs, c, tq, n_lat, span, windowed, mode, hps):
    refs = list(refs)
    pair = mode == "pair"
    sink_ref = refs.pop(0) if pair else None
    o_ref = refs.pop()
    q_ref = refs.pop(0)
    if mode == "mla":
        kv_ref, kpe_ref = refs
    hg = pl.program_id(1)
    qi = pl.program_id(2)

    parts = []
    if mode != "mla":
        parts.append((refs[0], refs[1], slice(None), None))
        if len(refs) == 4 and windowed:
            start = pl.multiple_of(jnp.clip(qi * tq - WINDOW, 0, n_lat - span), LANE)
            qpos = qi * tq + lax.broadcasted_iota(jnp.int32, (tq, span), 0)
            kpos = start + lax.broadcasted_iota(jnp.int32, (tq, span), 1)
            parts.append((refs[2], refs[3], pl.ds(start, span), jnp.abs(qpos - kpos) <= WINDOW))
        elif len(refs) == 4:
            parts.append((refs[2], refs[3], slice(None), None))

    def softmax_pv(q, ks, vs, sink, valids=(None,)):
        scores = []
        for k, valid in zip(ks, valids):
            s = lax.dot_general(q, k, (((1,), (1,)), ((), ())), preferred_element_type=F32)
            scores.append(s if valid is None else jnp.where(valid, s, NEG_INF))
        m = functools.reduce(jnp.maximum, [s.max(axis=-1, keepdims=True) for s in scores]) * c
        if sink is not None:
            m = jnp.maximum(m, sink * LOG2E)
        o = 0.0
        for s, v in zip(scores, vs):
            o = o + jnp.dot(jnp.exp2(s * c - m).astype(BF16), v, preferred_element_type=F32)
        return o, m

    if pair:
        lo_k = lax.broadcasted_iota(jnp.int32, (1, LANE), 1) < LANE // 2
        lo_q = lax.broadcasted_iota(jnp.int32, (tq, LANE), 1) < LANE // 2
        valids = [valid for _, _, _, valid in parts]
        ks = [k_ref[r, :] for k_ref, _, r, _ in parts]
        one = jnp.ones((), BF16)
        va = [jnp.where(lo_k, v_ref[r, :], one) for _, v_ref, r, _ in parts]
        vb = [jnp.where(lo_k, one, v_ref[r, :]) for _, v_ref, r, _ in parts]
        for t in range(hps):
            q = q_ref[:, t * LANE:(t + 1) * LANE]
            zero = jnp.zeros_like(q)
            head = (hg * hps + t) * 2
            sa, sb = sink_ref[head], sink_ref[head + 1]
            oa, ma = softmax_pv(jnp.where(lo_q, q, zero), ks, va, sa, valids)
            ob, mb = softmax_pv(jnp.where(lo_q, zero, q), ks, vb, sb, valids)
            da = pltpu.roll(oa, LANE // 2, axis=1) + jnp.exp2(sa * LOG2E - ma)
            db = pltpu.roll(ob, LANE // 2, axis=1) + jnp.exp2(sb * LOG2E - mb)
            o_ref[:, t * LANE:(t + 1) * LANE] = jnp.where(lo_q, oa / da, ob / db).astype(o_ref.dtype)
    else:
        if mode == "mla":
            kpe = kpe_ref[...]
        else:
            ks = [k_ref[...] for k_ref, _, _, _ in parts]
            vs = [jnp.concatenate([v_ref[...], jnp.ones(v_ref.shape, BF16)], axis=1)
                  for _, v_ref, _, _ in parts]
        for t in range(hps):
            if mode == "mla":
                q = q_ref[:, t * 2 * LANE:(t + 1) * 2 * LANE]
                ks = [jnp.concatenate([kv_ref[:, t * 2 * LANE:t * 2 * LANE + LANE], kpe], axis=1)]
                vs = [jnp.concatenate([kv_ref[:, t * 2 * LANE + LANE:(t + 1) * 2 * LANE],
                                       jnp.ones(kpe.shape, BF16)], axis=1)]
            else:
                q = q_ref[:, t * LANE:(t + 1) * LANE]
            o, _ = softmax_pv(q, ks, vs, None, (None,) * len(ks))
            o_ref[:, t * LANE:(t + 1) * LANE] = (o[:, :LANE] / o[:, LANE:]).astype(o_ref.dtype)


def attention(q, k, v, kpe, sink, *, mode, batch, n_q, n_k, q_row0, n_groups, hps, k_col0, v_col0,
              scale, tq, windowed=False, self_kv=None):
    tq = min(tq, n_q)
    nq = n_q // tq
    span = min(tq + 2 * WINDOW, n_q) if windowed else 0
    kern = functools.partial(_attn_kernel, c=scale * LOG2E, tq=tq, n_lat=n_q,
                             span=span, windowed=windowed, mode=mode, hps=hps)
    row0 = q_row0 // tq
    qw = hps * (2 * LANE if mode == "mla" else LANE)
    in_specs = [pl.BlockSpec((tq, qw), lambda b, g, i: (row0 + b * nq + i, g))]
    if mode == "mla":
        in_specs += [pl.BlockSpec((None, n_k, qw), lambda b, g, i: (b, 0, g)),
                     pl.BlockSpec((None, n_k, LANE), lambda b, g, i: (b, 0, 0))]
        args = [q, k, kpe]
    else:
        in_specs += [pl.BlockSpec((None, n_k, LANE), lambda b, g, i: (b, 0, k_col0 + g)),
                     pl.BlockSpec((None, n_k, LANE), lambda b, g, i: (b, 0, v_col0 + g))]
        args = [q, k, v]
        if self_kv is not None:
            assert q_row0 % n_q == 0
            seq0 = q_row0 // n_q
            in_specs += [pl.BlockSpec((n_q, LANE), lambda b, g, i: (seq0 + b, self_kv[0] + g)),
                         pl.BlockSpec((n_q, LANE), lambda b, g, i: (seq0 + b, self_kv[1] + g))]
            args += [q, q]
    if mode == "pair":
        in_specs.insert(0, pl.BlockSpec(memory_space=pltpu.SMEM))
        args.insert(0, sink)
    return pl.pallas_call(
        kern,
        grid=(batch, n_groups, nq),
        in_specs=in_specs,
        out_specs=pl.BlockSpec((tq, hps * LANE), lambda b, g, i: (b * nq + i, g)),
        out_shape=jax.ShapeDtypeStruct((batch * n_q, n_groups * hps * LANE), BF16),
        compiler_params=_cp("parallel", "parallel", "arbitrary"),
        name="attention",
    )(*args)


def kernel(x_prompt, x_sample, cache_mla_ckv, cache_mla_kpe, cache_gqa_k, cache_gqa_v, cache_swa_k,
           cache_swa_v, c, c_ctx, mod_w, mod_b, norm_g, ffn_w_in, ffn_conv_w, ffn_conv_b, ffn_w_out,
           mla_w_in, mla_q_norm_g, mla_kv_norm_g, mla_w_uq, mla_w_ukv, mla_w_o, gqa_w_qkv,
           gqa_q_norm_g, gqa_k_norm_g, gqa_w_o, swa_w_qkv, swa_sink, swa_w_o):
    bc, sc, d = x_prompt.shape
    bl, sl, _ = x_sample.shape
    depth = mod_w.shape[0]
    past = cache_mla_ckv.shape[2]
    rows = Rows(bc * sc, sc, bl * sl, sl)
    tc = rows.tc
    tm_row = min(256, tc, sl)
    tm_mm = min(1024, tc, sl)
    tm_prep = min(256, tc, sl)
    tn_mm = 1024
    tn_kv = 2048
    tn_o = 512
    tn_ffn = 256
    ffn_ch = min(256, sl // 2)
    head_ch = min(128, tm_mm)
    tq_dense, tq_win = 512, 512

    ql, kvl = mla_q_norm_g.shape[1], mla_kv_norm_g.shape[1]
    mla_rope = cache_mla_kpe.shape[3]
    mla_h = mla_w_o.shape[1] // LANE
    mla_qd = mla_w_uq.shape[2] // mla_h
    assert mla_qd - mla_rope == LANE and mla_w_ukv.shape[2] == mla_h * 2 * LANE
    mla_scale = float(mla_qd) ** -0.5
    mla_hps = 4 if mla_h % 4 == 0 else 1
    gqa_kvh, gqa_hd = cache_gqa_k.shape[3], cache_gqa_k.shape[4]
    gqa_h = gqa_w_o.shape[1] // gqa_hd
    gqa_group = gqa_h // gqa_kvh
    assert gqa_hd == LANE
    swa_kvh, swa_hd = cache_swa_k.shape[3], cache_swa_k.shape[4]
    swa_h = swa_sink.shape[1]
    swa_group = swa_h // swa_kvh
    assert 2 * swa_hd == LANE and swa_group % 2 == 0 and mla_rope == swa_hd

    x = Stream(x_prompt.reshape(tc, d), x_sample.reshape(bl * sl, d))
    cvec = jnp.concatenate([c_ctx[None], c, jnp.zeros((MOD_ROWS - 1 - bl, d), F32)], axis=0)
    mods = adaln(cvec, mod_w, mod_b).reshape(depth * MOD_ROWS * 6, 1, d)
    norm_g4 = norm_g.reshape(depth, 4, 1, d)

    tab64_prep = rope_tables(sl, swa_hd, tm_prep)
    tab64 = rope_tables(sl, swa_hd, tm_mm)
    tab128 = rope_tables(sl, LANE, tm_mm)
    no_gain = jnp.ones((2, LANE), F32)
    mla_w_o_b, gqa_w_o_b, swa_w_o_b = mla_w_o.astype(BF16), gqa_w_o.astype(BF16), swa_w_o.astype(BF16)
    gqa_w_qkv_b, mla_w_ukv_b = gqa_w_qkv.astype(BF16), mla_w_ukv.astype(BF16)

    outs = dict(ckv=[], kpe=[], gk=[], gv=[], sk=[], sv=[])
    mm = functools.partial(matmul, tm=tm_mm)

    h = normmod(x, norm_g4, mods, 0, 0, 0, rows, tm_row)
    for i in range(depth):
        kind, j = i % 3, i // 3

        if kind == 0:
            w_in = jnp.pad(mla_w_in[j], ((0, 0), (0, ql + kvl + LANE - mla_w_in.shape[2]))).astype(BF16)
            proj = mm(h, w_in, F32, tn=w_in.shape[1], single_w=True, tm=min(512, tm_mm))
            cq, ckv_f, ckv_b, kpe_f, kpe_b = mla_prep(proj, mla_q_norm_g[j], mla_kv_norm_g[j],
                                                      tab64_prep, rows, tm_prep)
            outs["ckv"].append(ckv_f[:tc].reshape(bc, sc, kvl))
            outs["kpe"].append(kpe_f[:tc, :mla_rope].reshape(bc, sc, mla_rope))
            w_uq = jnp.pad(mla_w_uq[j].reshape(ql, mla_h, mla_qd),
                           ((0, 0), (0, 0), (0, 2 * LANE - mla_qd))).reshape(ql, mla_h * 2 * LANE)
            tn_q = min(tn_mm, mla_h * 2 * LANE)
            q = proj_heads(cq, w_uq.astype(BF16), tab64, no_gain, rows, tm_mm, tn_q, head_ch,
                           rope_blocks=(False, True) * (tn_q // (2 * LANE)))
            kv_c = mm(ckv_b[:tc], mla_w_ukv_b, BF16, tn=tn_kv, layer=j).reshape(bc, sc, -1)
            kpe_c = kpe_b[:tc].reshape(bc, sc, LANE)
            mla_args = dict(mode="mla", n_groups=mla_h // mla_hps, hps=mla_hps, k_col0=0, v_col0=0,
                            scale=mla_scale, tq=tq_dense)
            a_c = attention(q, kv_c, None, kpe_c, None, batch=bc, n_q=sc, n_k=sc, q_row0=0, **mla_args)
            ckv_l = jnp.concatenate([cache_mla_ckv[:, j].astype(BF16), ckv_b[tc:].reshape(bl, sl, kvl)],
                                    axis=1)
            kpe_cache = jnp.pad(cache_mla_kpe[:, j], ((0, 0), (0, 0), (0, LANE - mla_rope))).astype(BF16)
            kpe_l = jnp.concatenate([kpe_cache, kpe_b[tc:].reshape(bl, sl, LANE)], axis=1)
            kv_l = mm(ckv_l.reshape(bl * (past + sl), kvl), mla_w_ukv_b, BF16, tn=tn_kv, layer=j,
                      tm=_row_tile(bl * (past + sl))).reshape(bl, past + sl, -1)
            a_l = attention(q, kv_l, None, kpe_l, None, batch=bl, n_q=sl, n_k=past + sl, q_row0=tc,
                            **mla_args)
            w_o = mla_w_o_b
        elif kind == 1:
            nq, nkv = gqa_h, gqa_kvh
            gains = jnp.stack([gqa_q_norm_g[j], gqa_k_norm_g[j]])
            tn_g = min(tn_mm, nkv * LANE)
            assert (nq * LANE) % tn_g == 0
            qk_tiles = (nq + nkv) * LANE // tn_g
            qkv = proj_heads(h, gqa_w_qkv_b, tab128, gains, rows, tm_mm, tn_g, head_ch, layer=j,
                             n_norm_tiles=qk_tiles, gain_switch=nq * LANE // tn_g, n_rope_tiles=qk_tiles)
            kv_f = mm(h, gqa_w_qkv_b, F32, tn=tn_g, layer=j, rows=tc, cols=(nq * LANE, 2 * nkv * LANE))
            k_f = head_norm(kv_f, gqa_k_norm_g[j], nkv * LANE, tm_prep)
            outs["gk"].append(k_f.reshape(bc, sc, nkv, gqa_hd))
            outs["gv"].append(kv_f[:, nkv * LANE:].reshape(bc, sc, nkv, gqa_hd))
            gqa_args = dict(mode="gqa", n_groups=nkv, hps=gqa_group, scale=float(gqa_hd) ** -0.5,
                            tq=tq_dense)
            k_c = qkv[:tc].reshape(bc, sc, -1)
            a_c = attention(qkv, k_c, k_c, None, None, batch=bc, n_q=sc, n_k=sc, q_row0=0,
                            k_col0=nq, v_col0=nq + nkv, **gqa_args)
            k_p = cache_gqa_k[:, j].reshape(bl, past, nkv * LANE).astype(BF16)
            v_p = cache_gqa_v[:, j].reshape(bl, past, nkv * LANE).astype(BF16)
            a_l = attention(qkv, k_p, v_p, None, None, batch=bl, n_q=sl, n_k=past, q_row0=tc,
                            k_col0=0, v_col0=0, self_kv=(nq, nq + nkv), **gqa_args)
            w_o = gqa_w_o_b
        else:
            nqb = swa_h // 2
            wq, wk, wv = jnp.split(swa_w_qkv[j], [swa_h * swa_hd, (swa_h + swa_kvh) * swa_hd], axis=1)
            dup = lambda m: jnp.tile(m.reshape(d, swa_kvh, 1, swa_hd), (1, 1, 2, 1)).reshape(d, -1)
            w_dup = jnp.concatenate([wq, dup(wk), dup(wv)], axis=1).astype(BF16)
            tn_s = min(tn_mm, swa_kvh * LANE)
            assert (nqb * LANE) % tn_s == 0
            qkv = proj_heads(h, w_dup, tab64, no_gain, rows, tm_mm, tn_s, head_ch,
                             n_rope_tiles=(nqb + swa_kvh) * LANE // tn_s)
            kv_f = mm(h, w_dup, F32, tn=tn_s, rows=tc, cols=(nqb * LANE, 2 * swa_kvh * LANE))
            kv_f = kv_f.reshape(bc, sc, 2, swa_kvh, 2, swa_hd)
            outs["sk"].append(kv_f[:, :, 0, :, 0])
            outs["sv"].append(kv_f[:, :, 1, :, 0])
            sink = swa_sink[j].astype(F32)
            swa_args = dict(mode="pair", n_groups=swa_kvh, hps=swa_group // 2,
                            scale=float(swa_hd) ** -0.5, tq=tq_win)
            k_c = qkv[:tc].reshape(bc, sc, -1)
            a_c = attention(qkv, k_c, k_c, None, sink, batch=bc, n_q=sc, n_k=sc, q_row0=0,
                            k_col0=nqb, v_col0=nqb + swa_kvh, **swa_args)
            dup_cache = lambda m: jnp.tile(m[:, :, :, None, :], (1, 1, 1, 2, 1)).reshape(
                bl, past, swa_kvh * LANE).astype(BF16)
            a_l = attention(qkv, dup_cache(cache_swa_k[:, j]), dup_cache(cache_swa_v[:, j]), None, sink,
                            batch=bl, n_q=sl, n_k=past, q_row0=tc, k_col0=0, v_col0=0, windowed=True,
                            self_kv=(nqb, nqb + swa_kvh), **swa_args)
            w_o = swa_w_o_b

        a = matmul2(a_c, a_l, w_o, BF16, tm=tm_mm, tn=tn_o, layer=j)
        x, h = resid_norm(x, a, norm_g4, mods, i, 1, 2, i, 2, 3, rows, tm_row)
        x = Stream(x)

        gated = ffn_in(h, ffn_w_in, ffn_conv_w, ffn_conv_b, i, rows, tm=rows.lat_seq, tn=tn_ffn, ch=ffn_ch)
        y = mm(gated, ffn_w_out, BF16, tn=tn_ffn, layer=i, single_x=True)
        if i + 1 < depth:
            x, h = resid_norm(x, y, norm_g4, mods, i, 3, 5, i + 1, 0, 0, rows, tm_row)
            x = Stream(x)
        else:
            y_ctx, y_lat = resid_split(x.parts[0], y, norm_g4, mods, i, 3, 5, rows, tm_row)

    stack = lambda xs: jnp.stack(xs, axis=1)
    return (y_ctx.reshape(bc, sc, d), y_lat.reshape(bl, sl, d),
            stack(outs["ckv"]), stack(outs["kpe"]), stack(outs["gk"]), stack(outs["gv"]),
            stack(outs["sk"]), stack(outs["sv"]))


def _row_tile(m):
    for tm in (1024, 512, 256, 128, 64, 32, 16, 8):
        if m % tm == 0:
            return tm
    return m
```

```python
import functools

import jax
import jax.numpy as jnp
from jax import lax
from jax.experimental import pallas as pl
from jax.experimental.pallas import tpu as pltpu

F32 = jnp.float32
BF16 = jnp.bfloat16

EPS = 1e-6
NEG_INF = -1e30
LOG2E = 1.4426950408889634
ROPE_BASE = 10000.0
GRID_W = 64
WINDOW = 128
LANE = 128
MOD_ROWS = 16
VMEM_LIMIT = 56 * 1024 * 1024


def _cp(*sem):
    return pltpu.CompilerParams(dimension_semantics=sem, vmem_limit_bytes=VMEM_LIMIT)


def _rms(x, g):
    ms = jnp.mean(x * x, axis=-1, keepdims=True)
    return x * lax.rsqrt(ms + EPS) * g


def _adaln_kernel(c_ref, w_ref, b_ref, o_ref):
    c = c_ref[...]
    s = c / (1.0 + jnp.exp(-c))
    o_ref[...] = jnp.dot(s.astype(BF16), w_ref[...].astype(BF16),
                         preferred_element_type=F32) + b_ref[...]


def adaln(cvec, mod_w, mod_b):
    n_layers, d, n = mod_w.shape
    tn = 512 if n % 512 == 0 else n
    return pl.pallas_call(
        _adaln_kernel,
        grid=(n_layers, n // tn),
        in_specs=[pl.BlockSpec((MOD_ROWS, d), lambda l, j: (0, 0)),
                  pl.BlockSpec((None, d, tn), lambda l, j: (l, 0, j)),
                  pl.BlockSpec((None, 1, tn), lambda l, j: (l, 0, j))],
        out_specs=pl.BlockSpec((None, MOD_ROWS, tn), lambda l, j: (l, 0, j)),
        out_shape=jax.ShapeDtypeStruct((n_layers, MOD_ROWS, n), F32),
        compiler_params=_cp("parallel", "parallel"),
        name="adaln",
    )(cvec, mod_w, mod_b.reshape(n_layers, 1, n))


class Rows:
    def __init__(self, n_ctx_rows, ctx_seq, n_lat_rows, lat_seq):
        self.tc, self.ctx_seq, self.tl, self.lat_seq = n_ctx_rows, ctx_seq, n_lat_rows, lat_seq
        self.t = n_ctx_rows + n_lat_rows

    def mod_row(self, i, tm):
        nctx = self.tc // tm
        per = self.lat_seq // tm
        return jnp.where(i < nctx, 0, 1 + (i - nctx) // per)


def _mod_spec(rows, tm, layer, comp, d):
    base = layer * MOD_ROWS * 6 + comp
    return pl.BlockSpec((None, 1, d), lambda i: (base + rows.mod_row(i, tm) * 6, 0, 0))


def _gain_spec(layer, which, d):
    return pl.BlockSpec((None, None, 1, d), lambda i: (layer, which, 0, 0))


class Stream:
    def __init__(self, *parts):
        self.parts = parts

    def specs(self, rows, tm, d):
        if len(self.parts) == 1:
            return [pl.BlockSpec((tm, d), lambda i: (i, 0))]
        na = rows.tc // tm
        return [pl.BlockSpec((tm, d), lambda i: (jnp.minimum(i, na - 1), 0)),
                pl.BlockSpec((tm, d), lambda i: (jnp.maximum(i - na, 0), 0))]


def _read_stream(refs, n_ctx_blocks):
    if len(refs) == 1:
        return refs[0][...]
    return jnp.where(pl.program_id(0) < n_ctx_blocks, refs[0][...], refs[1][...])


def _normmod_kernel(*refs, nx, n_ctx_blocks):
    g_ref, sh_ref, sc_ref, o_ref = refs[nx:]
    y = _rms(_read_stream(refs[:nx], n_ctx_blocks), g_ref[...])
    o_ref[...] = (y * (1.0 + sc_ref[...]) + sh_ref[...]).astype(o_ref.dtype)


def normmod(x, norm_g4, mods, layer, which, comp, rows, tm):
    d = norm_g4.shape[-1]
    nx = len(x.parts)
    return pl.pallas_call(
        functools.partial(_normmod_kernel, nx=nx, n_ctx_blocks=rows.tc // tm),
        grid=(rows.t // tm,),
        in_specs=x.specs(rows, tm, d) + [
            _gain_spec(layer, which, d),
            _mod_spec(rows, tm, layer, comp, d), _mod_spec(rows, tm, layer, comp + 1, d)],
        out_specs=pl.BlockSpec((tm, d), lambda i: (i, 0)),
        out_shape=jax.ShapeDtypeStruct((rows.t, d), BF16),
        compiler_params=_cp("parallel"),
        name="normmod",
    )(*x.parts, norm_g4, mods, mods)


def _resid_kernel(x_ref, a_ref, g_ref, gate_ref, oc_ref, ol_ref, *, n_ctx_blocks):
    x = x_ref[...] + gate_ref[...] * _rms(a_ref[...].astype(F32), g_ref[...])

    @pl.when(pl.program_id(0) < n_ctx_blocks)
    def _():
        oc_ref[...] = x

    @pl.when(pl.program_id(0) >= n_ctx_blocks)
    def _():
        ol_ref[...] = x


def resid_split(x, a, norm_g4, mods, layer, which, comp, rows, tm):
    t, d = x.shape
    na = rows.tc // tm
    row = pl.BlockSpec((tm, d), lambda i: (i, 0))
    return pl.pallas_call(
        functools.partial(_resid_kernel, n_ctx_blocks=na),
        grid=(t // tm,),
        in_specs=[row, row, _gain_spec(layer, which, d), _mod_spec(rows, tm, layer, comp, d)],
        out_specs=[pl.BlockSpec((tm, d), lambda i: (jnp.minimum(i, na - 1), 0)),
                   pl.BlockSpec((tm, d), lambda i: (jnp.maximum(i - na, 0), 0))],
        out_shape=[jax.ShapeDtypeStruct((rows.tc, d), F32), jax.ShapeDtypeStruct((rows.tl, d), F32)],
        compiler_params=_cp("arbitrary"),
        name="resid",
    )(x, a, norm_g4, mods)


def _resid_norm_kernel(*refs, nx, n_ctx_blocks):
    a_ref, g_ref, gate_ref, g2_ref, sh_ref, sc_ref, o_ref, h_ref = refs[nx:]
    x = _read_stream(refs[:nx], n_ctx_blocks)
    x = x + gate_ref[...] * _rms(a_ref[...].astype(F32), g_ref[...])
    o_ref[...] = x
    h_ref[...] = (_rms(x, g2_ref[...]) * (1.0 + sc_ref[...]) + sh_ref[...]).astype(h_ref.dtype)


def resid_norm(x, a, norm_g4, mods, layer, which, comp, nxt_layer, nxt_which, nxt_comp, rows, tm):
    t, d = a.shape
    nx = len(x.parts)
    row = pl.BlockSpec((tm, d), lambda i: (i, 0))
    return pl.pallas_call(
        functools.partial(_resid_norm_kernel, nx=nx, n_ctx_blocks=rows.tc // tm),
        grid=(t // tm,),
        in_specs=x.specs(rows, tm, d) + [
            row, _gain_spec(layer, which, d), _mod_spec(rows, tm, layer, comp, d),
            _gain_spec(nxt_layer, nxt_which, d),
            _mod_spec(rows, tm, nxt_layer, nxt_comp, d),
            _mod_spec(rows, tm, nxt_layer, nxt_comp + 1, d)],
        out_specs=[row, row],
        out_shape=[jax.ShapeDtypeStruct((t, d), F32), jax.ShapeDtypeStruct((t, d), BF16)],
        compiler_params=_cp("parallel"),
        name="resid_norm",
    )(*x.parts, a, norm_g4, mods, norm_g4, mods, mods)


def _mm_kernel(x_ref, w_ref, o_ref):
    o_ref[...] = jnp.dot(x_ref[...], w_ref[...].astype(BF16),
                         preferred_element_type=F32).astype(o_ref.dtype)


def _mm2_kernel(xa_ref, xb_ref, w_ref, o_ref, *, na):
    @pl.when(pl.program_id(0) < na)
    def _():
        o_ref[...] = jnp.dot(xa_ref[...], w_ref[...], preferred_element_type=F32).astype(o_ref.dtype)

    @pl.when(pl.program_id(0) >= na)
    def _():
        o_ref[...] = jnp.dot(xb_ref[...], w_ref[...], preferred_element_type=F32).astype(o_ref.dtype)


def _w_spec(w, layer, k, tn, **mode):
    if w.ndim == 3:
        return pl.BlockSpec((None, k, tn), lambda i, j: (layer, 0, j), **mode)
    return pl.BlockSpec((k, tn), lambda i, j: (0, j), **mode)


def matmul(x, w, out_dtype, tm, tn, layer=0, single_x=False, single_w=False, rows=None, cols=None):
    m, k = x.shape
    m = m if rows is None else rows
    c0, n = (0, w.shape[-1]) if cols is None else cols
    tm, tn = min(tm, m), min(tn, n)
    assert m % tm == 0 and n % tn == 0 and c0 % tn == 0, (m, n, c0, tm, tn)
    j0 = c0 // tn
    x_mode = dict(pipeline_mode=pl.Buffered(1)) if single_x else {}
    w_mode = dict(pipeline_mode=pl.Buffered(1)) if single_w else {}
    if w.ndim == 3:
        w_spec = pl.BlockSpec((None, k, tn), lambda i, j: (layer, 0, j0 + j), **w_mode)
    else:
        w_spec = pl.BlockSpec((k, tn), lambda i, j: (0, j0 + j), **w_mode)
    return pl.pallas_call(
        _mm_kernel,
        grid=(m // tm, n // tn),
        in_specs=[pl.BlockSpec((tm, k), lambda i, j: (i, 0), **x_mode), w_spec],
        out_specs=pl.BlockSpec((tm, tn), lambda i, j: (i, j)),
        out_shape=jax.ShapeDtypeStruct((m, n), out_dtype),
        compiler_params=_cp("parallel", "parallel"),
        name="matmul",
    )(x, w)


def matmul2(xa, xb, w, out_dtype, tm, tn, layer=0):
    ma, k = xa.shape
    mb = xb.shape[0]
    n = w.shape[-1]
    tn = min(tn, n)
    assert ma % tm == 0 and mb % tm == 0 and n % tn == 0, (ma, mb, n, tm, tn)
    na = ma // tm
    return pl.pallas_call(
        functools.partial(_mm2_kernel, na=na),
        grid=((ma + mb) // tm, n // tn),
        in_specs=[pl.BlockSpec((tm, k), lambda i, j: (jnp.minimum(i, na - 1), 0)),
                  pl.BlockSpec((tm, k), lambda i, j: (jnp.maximum(i - na, 0), 0)),
                  _w_spec(w, layer, k, tn)],
        out_specs=pl.BlockSpec((tm, tn), lambda i, j: (i, j)),
        out_shape=jax.ShapeDtypeStruct((ma + mb, n), out_dtype),
        compiler_params=_cp("parallel", "parallel"),
        name="matmul2",
    )(xa, xb, w)


def _ffn_in_kernel(h_ref, wg_ref, wv_ref, cwg_ref, cwv_ref, cbg_ref, cbv_ref, o_ref, *,
                   n_ctx_blocks, ctx_seq, lat_seq, ch):
    tm, tn = o_ref.shape
    nch = tm // ch
    seq = jnp.where(pl.program_id(0) < n_ctx_blocks, ctx_seq, lat_seq)
    row = lax.broadcasted_iota(jnp.int32, (ch, tn), 0)
    wg = wg_ref[...].astype(BF16)
    wv = wv_ref[...].astype(BF16)
    ug = [jnp.dot(h_ref[r * ch:(r + 1) * ch, :], wg, preferred_element_type=F32) for r in range(nch)]
    uv = [jnp.dot(h_ref[r * ch:(r + 1) * ch, :], wv, preferred_element_type=F32) for r in range(nch)]

    def conv(us, r, cw_ref, cb_ref):
        u = us[r]
        pos = (row + r * ch) & (seq - 1)
        prev = pltpu.roll(u, 1, axis=0)
        if r > 0:
            prev = jnp.where(row == 0, us[r - 1][ch - 1:ch, :], prev)
        prev = jnp.where(pos == 0, 0.0, prev)
        nxt = pltpu.roll(u, ch - 1, axis=0)
        if r < nch - 1:
            nxt = jnp.where(row == ch - 1, us[r + 1][0:1, :], nxt)
        nxt = jnp.where(pos == seq - 1, 0.0, nxt)
        return prev * cw_ref[0:1, :] + u * cw_ref[1:2, :] + nxt * cw_ref[2:3, :] + cb_ref[...]

    for r in range(nch):
        gate = conv(ug, r, cwg_ref, cbg_ref)
        val = conv(uv, r, cwv_ref, cbv_ref)
        o_ref[r * ch:(r + 1) * ch, :] = (gate / (1.0 + jnp.exp(-gate)) * val).astype(o_ref.dtype)


def ffn_in(h, w_in, conv_w, conv_b, layer, rows, tm, tn, ch):
    t, d = h.shape
    f = w_in.shape[2] // 2
    assert f % tn == 0 and rows.tc % tm == 0 and tm % rows.ctx_seq == 0 and tm == rows.lat_seq
    assert rows.ctx_seq & (rows.ctx_seq - 1) == 0 and rows.lat_seq & (rows.lat_seq - 1) == 0
    assert tm % ch == 0
    nj = f // tn
    kern = functools.partial(_ffn_in_kernel, n_ctx_blocks=rows.tc // tm,
                             ctx_seq=rows.ctx_seq, lat_seq=rows.lat_seq, ch=ch)
    cb = conv_b.reshape(conv_b.shape[0], 1, 2 * f)
    return pl.pallas_call(
        kern,
        grid=(t // tm, nj),
        in_specs=[pl.BlockSpec((tm, d), lambda i, j: (i, 0), pipeline_mode=pl.Buffered(1)),
                  pl.BlockSpec((None, d, tn), lambda i, j: (layer, 0, j)),
                  pl.BlockSpec((None, d, tn), lambda i, j: (layer, 0, j + nj)),
                  pl.BlockSpec((None, 3, tn), lambda i, j: (layer, 0, j)),
                  pl.BlockSpec((None, 3, tn), lambda i, j: (layer, 0, j + nj)),
                  pl.BlockSpec((None, 1, tn), lambda i, j: (layer, 0, j)),
                  pl.BlockSpec((None, 1, tn), lambda i, j: (layer, 0, j + nj))],
        out_specs=pl.BlockSpec((tm, tn), lambda i, j: (i, j)),
        out_shape=jax.ShapeDtypeStruct((t, f), BF16),
        compiler_params=_cp("parallel", "parallel"),
        name="ffn_in",
    )(h, w_in, w_in, conv_w, conv_w, cb, cb)


def rope_tables(n_lat, head_dim, tm):
    d2 = head_dim // 2
    half = d2 // 2
    lane = jnp.arange(LANE)
    freqs = ROPE_BASE ** (-jnp.arange(half, dtype=F32) / half)
    f = freqs[lane % half]
    n = jnp.arange(n_lat)
    pos = jnp.where(((lane // d2) % 2 == 0)[None, :], (n // GRID_W)[:, None], (n % GRID_W)[:, None])
    ang = pos.astype(F32) * f[None, :]
    sign = jnp.where((lane % d2) < half, -1.0, 1.0).astype(F32)
    cos = jnp.concatenate([jnp.ones((tm, LANE), F32), jnp.cos(ang)], axis=0)
    sin = jnp.concatenate([jnp.zeros((tm, LANE), F32), jnp.sin(ang) * sign[None, :]], axis=0)
    return cos, sin, half


def _rope(x, cos, sin, half):
    lane = lax.broadcasted_iota(jnp.int32, x.shape, 1)
    partner = jnp.where((lane & (2 * half - 1)) < half, pltpu.roll(x, LANE - half, axis=1),
                        pltpu.roll(x, half, axis=1))
    return x * cos + partner * sin


def _table_spec(rows, tm):
    nctx = rows.tc // tm
    per = rows.lat_seq // tm
    return lambda i: (jnp.where(i < nctx, 0, 1 + (i - nctx) % per), 0)


def _head_norm_kernel(x_ref, g_ref, o_ref):
    for c in range(o_ref.shape[1] // LANE):
        cols = slice(c * LANE, (c + 1) * LANE)
        o_ref[:, cols] = _rms(x_ref[:, cols], g_ref[...])


def head_norm(x, g, width, tm):
    m = x.shape[0]
    return pl.pallas_call(
        _head_norm_kernel,
        grid=(m // tm,),
        in_specs=[pl.BlockSpec((tm, width), lambda i: (i, 0)), pl.BlockSpec((1, LANE), lambda i: (0, 0))],
        out_specs=pl.BlockSpec((tm, width), lambda i: (i, 0)),
        out_shape=jax.ShapeDtypeStruct((m, width), F32),
        compiler_params=_cp("parallel"),
        name="head_norm",
    )(x, g.reshape(1, LANE))


def _proj_heads_kernel(x_ref, w_ref, cos_ref, sin_ref, g_ref, o_ref, *,
                       ch, half, rope_blocks, n_norm_tiles):
    tm, tn = o_ref.shape
    w = w_ref[...]
    normed = pl.program_id(1) < n_norm_tiles
    for r in range(tm // ch):
        rs = slice(r * ch, (r + 1) * ch)
        acc = jnp.dot(x_ref[rs, :], w, preferred_element_type=F32)
        cos, sin = cos_ref[rs, :], sin_ref[rs, :]
        for c in range(tn // LANE):
            x = acc[:, c * LANE:(c + 1) * LANE]
            if n_norm_tiles:
                x = jnp.where(normed, _rms(x, g_ref[...]), x)
            if rope_blocks[c]:
                x = _rope(x, cos, sin, half)
            o_ref[rs, c * LANE:(c + 1) * LANE] = x.astype(o_ref.dtype)


def proj_heads(x, w, tables, gains, rows, tm, tn, ch, *, layer=0, n_norm_tiles=0, gain_switch=0,
               rope_blocks=None, n_rope_tiles=None):
    cos, sin, half = tables
    m, k = x.shape
    n = w.shape[-1]
    nt = n // tn
    assert m % tm == 0 and n % tn == 0 and tm % ch == 0
    rope_blocks = tuple(rope_blocks or (True,) * (tn // LANE))
    n_rope_tiles = nt if n_rope_tiles is None else n_rope_tiles
    nctx = rows.tc // tm
    per = rows.lat_seq // tm
    tspec = lambda i, j: (jnp.where((i < nctx) | (j >= n_rope_tiles), 0, 1 + (i - nctx) % per), 0)
    return pl.pallas_call(
        functools.partial(_proj_heads_kernel, ch=ch, half=half, rope_blocks=rope_blocks,
                          n_norm_tiles=n_norm_tiles),
        grid=(m // tm, nt),
        in_specs=[pl.BlockSpec((tm, k), lambda i, j: (i, 0)), _w_spec(w, layer, k, tn),
                  pl.BlockSpec((tm, LANE), tspec), pl.BlockSpec((tm, LANE), tspec),
                  pl.BlockSpec((None, 1, LANE), lambda i, j: (jnp.where(j < gain_switch, 0, 1), 0, 0))],
        out_specs=pl.BlockSpec((tm, tn), lambda i, j: (i, j)),
        out_shape=jax.ShapeDtypeStruct((m, n), BF16),
        compiler_params=_cp("parallel", "parallel"),
        name="proj_heads",
    )(x, w, cos, sin, gains.reshape(2, 1, LANE))


def _mla_prep_kernel(x_ref, cos_ref, sin_ref, gq_ref, gkv_ref,
                     cq_ref, ckvf_ref, ckvb_ref, kpef_ref, kpeb_ref, *, ql, kvl, half):
    cq_ref[...] = _rms(x_ref[:, :ql], gq_ref[...]).astype(BF16)
    ckv = _rms(x_ref[:, ql:ql + kvl], gkv_ref[...])
    ckvf_ref[...] = ckv
    ckvb_ref[...] = ckv.astype(BF16)
    kpe = x_ref[:, ql + kvl:ql + kvl + LANE]
    kpef_ref[...] = kpe
    kpeb_ref[...] = _rope(kpe, cos_ref[...], sin_ref[...], half).astype(BF16)


def mla_prep(x, gq, gkv, tables, rows, tm):
    cos, sin, half = tables
    ql, kvl = gq.shape[0], gkv.shape[0]
    t, width = x.shape
    tspec = _table_spec(rows, tm)
    row = lambda i: (i, 0)
    return pl.pallas_call(
        functools.partial(_mla_prep_kernel, ql=ql, kvl=kvl, half=half),
        grid=(t // tm,),
        in_specs=[pl.BlockSpec((tm, width), row),
                  pl.BlockSpec((tm, LANE), tspec),
                  pl.BlockSpec((tm, LANE), tspec),
                  pl.BlockSpec((1, ql), lambda i: (0, 0)),
                  pl.BlockSpec((1, kvl), lambda i: (0, 0))],
        out_specs=[pl.BlockSpec((tm, ql), row), pl.BlockSpec((tm, kvl), row),
                   pl.BlockSpec((tm, kvl), row), pl.BlockSpec((tm, LANE), row),
                   pl.BlockSpec((tm, LANE), row)],
        out_shape=[jax.ShapeDtypeStruct((t, ql), BF16), jax.ShapeDtypeStruct((t, kvl), F32),
                   jax.ShapeDtypeStruct((t, kvl), BF16), jax.ShapeDtypeStruct((t, LANE), F32),
                   jax.ShapeDtypeStruct((t, LANE), BF16)],
        compiler_params=_cp("parallel"),
        name="mla_prep",
    )(x, cos, sin, gq.reshape(1, ql), gkv.reshape(1, kvl))


def _attn_kernel(*refs, c, tq, n_lat, span, windowed, mode, hps):
    refs = list(refs)
    pair = mode == "pair"
    sink_ref = refs.pop(0) if pair else None
    o_ref = refs.pop()
    q_ref = refs.pop(0)
    if mode == "mla":
        kv_ref, kpe_ref = refs
    hg = pl.program_id(1)
    qi = pl.program_id(2)

    parts = []
    if mode != "mla":
        parts.append((refs[0], refs[1], slice(None), None))
        if len(refs) == 4 and windowed:
            start = pl.multiple_of(jnp.clip(qi * tq - WINDOW, 0, n_lat - span), LANE)
            qpos = qi * tq + lax.broadcasted_iota(jnp.int32, (tq, span), 0)
            kpos = start + lax.broadcasted_iota(jnp.int32, (tq, span), 1)
            parts.append((refs[2], refs[3], pl.ds(start, span), jnp.abs(qpos - kpos) <= WINDOW))
        elif len(refs) == 4:
            parts.append((refs[2], refs[3], slice(None), None))

    def softmax_pv(q, ks, vs, sink, valids=(None,)):
        scores = []
        for k, valid in zip(ks, valids):
            s = lax.dot_general(q, k, (((1,), (1,)), ((), ())), preferred_element_type=F32)
            scores.append(s if valid is None else jnp.where(valid, s, NEG_INF))
        m = functools.reduce(jnp.maximum, [s.max(axis=-1, keepdims=True) for s in scores]) * c
        if sink is not None:
            m = jnp.maximum(m, sink * LOG2E)
        o = 0.0
        for s, v in zip(scores, vs):
            o = o + jnp.dot(jnp.exp2(s * c - m).astype(BF16), v, preferred_element_type=F32)
        return o, m

    if pair:
        lo_k = lax.broadcasted_iota(jnp.int32, (1, LANE), 1) < LANE // 2
        lo_q = lax.broadcasted_iota(jnp.int32, (tq, LANE), 1) < LANE // 2
        valids = [valid for _, _, _, valid in parts]
        ks = [k_ref[r, :] for k_ref, _, r, _ in parts]
        one = jnp.ones((), BF16)
        va = [jnp.where(lo_k, v_ref[r, :], one) for _, v_ref, r, _ in parts]
        vb = [jnp.where(lo_k, one, v_ref[r, :]) for _, v_ref, r, _ in parts]
        for t in range(hps):
            q = q_ref[:, t * LANE:(t + 1) * LANE]
            zero = jnp.zeros_like(q)
            head = (hg * hps + t) * 2
            sa, sb = sink_ref[head], sink_ref[head + 1]
            oa, ma = softmax_pv(jnp.where(lo_q, q, zero), ks, va, sa, valids)
            ob, mb = softmax_pv(jnp.where(lo_q, zero, q), ks, vb, sb, valids)
            da = pltpu.roll(oa, LANE // 2, axis=1) + jnp.exp2(sa * LOG2E - ma)
            db = pltpu.roll(ob, LANE // 2, axis=1) + jnp.exp2(sb * LOG2E - mb)
            o_ref[:, t * LANE:(t + 1) * LANE] = jnp.where(lo_q, oa / da, ob / db).astype(o_ref.dtype)
    else:
        if mode == "mla":
            kpe = kpe_ref[...]
        else:
            ks = [k_ref[...] for k_ref, _, _, _ in parts]
            vs = [jnp.concatenate([v_ref[...], jnp.ones(v_ref.shape, BF16)], axis=1)
                  for _, v_ref, _, _ in parts]
        for t in range(hps):
            if mode == "mla":
                q = q_ref[:, t * 2 * LANE:(t + 1) * 2 * LANE]
                ks = [jnp.concatenate([kv_ref[:, t * 2 * LANE:t * 2 * LANE + LANE], kpe], axis=1)]
                vs = [jnp.concatenate([kv_ref[:, t * 2 * LANE + LANE:(t + 1) * 2 * LANE],
                                       jnp.ones(kpe.shape, BF16)], axis=1)]
            else:
                q = q_ref[:, t * LANE:(t + 1) * LANE]
            o, _ = softmax_pv(q, ks, vs, None, (None,) * len(ks))
            o_ref[:, t * LANE:(t + 1) * LANE] = (o[:, :LANE] / o[:, LANE:]).astype(o_ref.dtype)


def attention(q, k, v, kpe, sink, *, mode, batch, n_q, n_k, q_row0, n_groups, hps, k_col0, v_col0,
              scale, tq, windowed=False, self_kv=None):
    tq = min(tq, n_q)
    nq = n_q // tq
    span = min(tq + 2 * WINDOW, n_q) if windowed else 0
    kern = functools.partial(_attn_kernel, c=scale * LOG2E, tq=tq, n_lat=n_q,
                             span=span, windowed=windowed, mode=mode, hps=hps)
    row0 = q_row0 // tq
    qw = hps * (2 * LANE if mode == "mla" else LANE)
    in_specs = [pl.BlockSpec((tq, qw), lambda b, g, i: (row0 + b * nq + i, g))]
    if mode == "mla":
        in_specs += [pl.BlockSpec((None, n_k, qw), lambda b, g, i: (b, 0, g)),
                     pl.BlockSpec((None, n_k, LANE), lambda b, g, i: (b, 0, 0))]
        args = [q, k, kpe]
    else:
        in_specs += [pl.BlockSpec((None, n_k, LANE), lambda b, g, i: (b, 0, k_col0 + g)),
                     pl.BlockSpec((None, n_k, LANE), lambda b, g, i: (b, 0, v_col0 + g))]
        args = [q, k, v]
        if self_kv is not None:
            assert q_row0 % n_q == 0
            seq0 = q_row0 // n_q
            in_specs += [pl.BlockSpec((n_q, LANE), lambda b, g, i: (seq0 + b, self_kv[0] + g)),
                         pl.BlockSpec((n_q, LANE), lambda b, g, i: (seq0 + b, self_kv[1] + g))]
            args += [q, q]
    if mode == "pair":
        in_specs.insert(0, pl.BlockSpec(memory_space=pltpu.SMEM))
        args.insert(0, sink)
    return pl.pallas_call(
        kern,
        grid=(batch, n_groups, nq),
        in_specs=in_specs,
        out_specs=pl.BlockSpec((tq, hps * LANE), lambda b, g, i: (b * nq + i, g)),
        out_shape=jax.ShapeDtypeStruct((batch * n_q, n_groups * hps * LANE), BF16),
        compiler_params=_cp("parallel", "parallel", "arbitrary"),
        name="attention",
    )(*args)


def kernel(x_prompt, x_sample, cache_mla_ckv, cache_mla_kpe, cache_gqa_k, cache_gqa_v, cache_swa_k,
           cache_swa_v, c, c_ctx, mod_w, mod_b, norm_g, ffn_w_in, ffn_conv_w, ffn_conv_b, ffn_w_out,
           mla_w_in, mla_q_norm_g, mla_kv_norm_g, mla_w_uq, mla_w_ukv, mla_w_o, gqa_w_qkv,
           gqa_q_norm_g, gqa_k_norm_g, gqa_w_o, swa_w_qkv, swa_sink, swa_w_o):
    bc, sc, d = x_prompt.shape
    bl, sl, _ = x_sample.shape
    depth = mod_w.shape[0]
    past = cache_mla_ckv.shape[2]
    rows = Rows(bc * sc, sc, bl * sl, sl)
    tc = rows.tc
    tm_row = min(256, tc, sl)
    tm_mm = min(1024, tc, sl)
    tm_prep = min(256, tc, sl)
    tn_mm = 1024
    tn_kv = 2048
    tn_o = 512
    tn_ffn = 256
    ffn_ch = min(256, sl // 2)
    head_ch = min(128, tm_mm)
    tq_mla, tq_gqa, tq_win = 512, 1024, 512

    ql, kvl = mla_q_norm_g.shape[1], mla_kv_norm_g.shape[1]
    mla_rope = cache_mla_kpe.shape[3]
    mla_h = mla_w_o.shape[1] // LANE
    mla_qd = mla_w_uq.shape[2] // mla_h
    assert mla_qd - mla_rope == LANE and mla_w_ukv.shape[2] == mla_h * 2 * LANE
    mla_scale = float(mla_qd) ** -0.5
    mla_hps = next(n for n in (8, 4, 2, 1) if mla_h % n == 0)
    gqa_kvh, gqa_hd = cache_gqa_k.shape[3], cache_gqa_k.shape[4]
    gqa_h = gqa_w_o.shape[1] // gqa_hd
    gqa_group = gqa_h // gqa_kvh
    assert gqa_hd == LANE
    swa_kvh, swa_hd = cache_swa_k.shape[3], cache_swa_k.shape[4]
    swa_h = swa_sink.shape[1]
    swa_group = swa_h // swa_kvh
    assert 2 * swa_hd == LANE and swa_group % 2 == 0 and mla_rope == swa_hd

    x = Stream(x_prompt.reshape(tc, d), x_sample.reshape(bl * sl, d))
    cvec = jnp.concatenate([c_ctx[None], c, jnp.zeros((MOD_ROWS - 1 - bl, d), F32)], axis=0)
    mods = adaln(cvec, mod_w, mod_b).reshape(depth * MOD_ROWS * 6, 1, d)
    norm_g4 = norm_g.reshape(depth, 4, 1, d)

    tab64_prep = rope_tables(sl, swa_hd, tm_prep)
    tab64 = rope_tables(sl, swa_hd, tm_mm)
    tab128 = rope_tables(sl, LANE, tm_mm)
    no_gain = jnp.ones((2, LANE), F32)
    mla_w_o_b, gqa_w_o_b, swa_w_o_b = mla_w_o.astype(BF16), gqa_w_o.astype(BF16), swa_w_o.astype(BF16)
    gqa_w_qkv_b, mla_w_ukv_b = gqa_w_qkv.astype(BF16), mla_w_ukv.astype(BF16)

    outs = dict(ckv=[], kpe=[], gk=[], gv=[], sk=[], sv=[])
    mm = functools.partial(matmul, tm=tm_mm)

    h = normmod(x, norm_g4, mods, 0, 0, 0, rows, tm_row)
    for i in range(depth):
        kind, j = i % 3, i // 3

        if kind == 0:
            w_in = jnp.pad(mla_w_in[j], ((0, 0), (0, ql + kvl + LANE - mla_w_in.shape[2]))).astype(BF16)
            proj = mm(h, w_in, F32, tn=w_in.shape[1], single_w=True, tm=min(512, tm_mm))
            cq, ckv_f, ckv_b, kpe_f, kpe_b = mla_prep(proj, mla_q_norm_g[j], mla_kv_norm_g[j],
                                                      tab64_prep, rows, tm_prep)
            outs["ckv"].append(ckv_f[:tc].reshape(bc, sc, kvl))
            outs["kpe"].append(kpe_f[:tc, :mla_rope].reshape(bc, sc, mla_rope))
            w_uq = jnp.pad(mla_w_uq[j].reshape(ql, mla_h, mla_qd),
                           ((0, 0), (0, 0), (0, 2 * LANE - mla_qd))).reshape(ql, mla_h * 2 * LANE)
            tn_q = min(tn_mm, mla_h * 2 * LANE)
            q = proj_heads(cq, w_uq.astype(BF16), tab64, no_gain, rows, tm_mm, tn_q, head_ch,
                           rope_blocks=(False, True) * (tn_q // (2 * LANE)))
            kv_c = mm(ckv_b[:tc], mla_w_ukv_b, BF16, tn=tn_kv, layer=j).reshape(bc, sc, -1)
            kpe_c = kpe_b[:tc].reshape(bc, sc, LANE)
            mla_args = dict(mode="mla", n_groups=mla_h // mla_hps, hps=mla_hps, k_col0=0, v_col0=0,
                            scale=mla_scale, tq=tq_mla)
            a_c = attention(q, kv_c, None, kpe_c, None, batch=bc, n_q=sc, n_k=sc, q_row0=0, **mla_args)
            ckv_l = jnp.concatenate([cache_mla_ckv[:, j].astype(BF16), ckv_b[tc:].reshape(bl, sl, kvl)],
                                    axis=1)
            kpe_cache = jnp.pad(cache_mla_kpe[:, j], ((0, 0), (0, 0), (0, LANE - mla_rope))).astype(BF16)
            kpe_l = jnp.concatenate([kpe_cache, kpe_b[tc:].reshape(bl, sl, LANE)], axis=1)
            kv_l = mm(ckv_l.reshape(bl * (past + sl), kvl), mla_w_ukv_b, BF16, tn=tn_kv, layer=j,
                      tm=_row_tile(bl * (past + sl))).reshape(bl, past + sl, -1)
            a_l = attention(q, kv_l, None, kpe_l, None, batch=bl, n_q=sl, n_k=past + sl, q_row0=tc,
                            **mla_args)
            w_o = mla_w_o_b
        elif kind == 1:
            nq, nkv = gqa_h, gqa_kvh
            gains = jnp.stack([gqa_q_norm_g[j], gqa_k_norm_g[j]])
            tn_g = min(tn_mm, nkv * LANE)
            assert (nq * LANE) % tn_g == 0
            qk_tiles = (nq + nkv) * LANE // tn_g
            qkv = proj_heads(h, gqa_w_qkv_b, tab128, gains, rows, tm_mm, tn_g, head_ch, layer=j,
                             n_norm_tiles=qk_tiles, gain_switch=nq * LANE // tn_g, n_rope_tiles=qk_tiles)
            kv_f = mm(h, gqa_w_qkv_b, F32, tn=tn_g, layer=j, rows=tc, cols=(nq * LANE, 2 * nkv * LANE))
            k_f = head_norm(kv_f, gqa_k_norm_g[j], nkv * LANE, tm_prep)
            outs["gk"].append(k_f.reshape(bc, sc, nkv, gqa_hd))
            outs["gv"].append(kv_f[:, nkv * LANE:].reshape(bc, sc, nkv, gqa_hd))
            gqa_args = dict(mode="gqa", n_groups=nkv, hps=gqa_group, scale=float(gqa_hd) ** -0.5,
                            tq=tq_gqa)
            k_c = qkv[:tc].reshape(bc, sc, -1)
            a_c = attention(qkv, k_c, k_c, None, None, batch=bc, n_q=sc, n_k=sc, q_row0=0,
                            k_col0=nq, v_col0=nq + nkv, **gqa_args)
            k_p = cache_gqa_k[:, j].reshape(bl, past, nkv * LANE).astype(BF16)
            v_p = cache_gqa_v[:, j].reshape(bl, past, nkv * LANE).astype(BF16)
            a_l = attention(qkv, k_p, v_p, None, None, batch=bl, n_q=sl, n_k=past, q_row0=tc,
                            k_col0=0, v_col0=0, self_kv=(nq, nq + nkv), **gqa_args)
            w_o = gqa_w_o_b
        else:
            nqb = swa_h // 2
            wq, wk, wv = jnp.split(swa_w_qkv[j], [swa_h * swa_hd, (swa_h + swa_kvh) * swa_hd], axis=1)
            dup = lambda m: jnp.tile(m.reshape(d, swa_kvh, 1, swa_hd), (1, 1, 2, 1)).reshape(d, -1)
            w_dup = jnp.concatenate([wq, dup(wk), dup(wv)], axis=1).astype(BF16)
            tn_s = min(tn_mm, swa_kvh * LANE)
            assert (nqb * LANE) % tn_s == 0
            qkv = proj_heads(h, w_dup, tab64, no_gain, rows, tm_mm, tn_s, head_ch,
                             n_rope_tiles=(nqb + swa_kvh) * LANE // tn_s)
            kv_f = mm(h, w_dup, F32, tn=tn_s, rows=tc, cols=(nqb * LANE, 2 * swa_kvh * LANE))
            kv_f = kv_f.reshape(bc, sc, 2, swa_kvh, 2, swa_hd)
            outs["sk"].append(kv_f[:, :, 0, :, 0])
            outs["sv"].append(kv_f[:, :, 1, :, 0])
            sink = swa_sink[j].astype(F32)
            swa_args = dict(mode="pair", n_groups=swa_kvh, hps=swa_group // 2,
                            scale=float(swa_hd) ** -0.5, tq=tq_win)
            k_c = qkv[:tc].reshape(bc, sc, -1)
            a_c = attention(qkv, k_c, k_c, None, sink, batch=bc, n_q=sc, n_k=sc, q_row0=0,
                            k_col0=nqb, v_col0=nqb + swa_kvh, **swa_args)
            dup_cache = lambda m: jnp.tile(m[:, :, :, None, :], (1, 1, 1, 2, 1)).reshape(
                bl, past, swa_kvh * LANE).astype(BF16)
            a_l = attention(qkv, dup_cache(cache_swa_k[:, j]), dup_cache(cache_swa_v[:, j]), None, sink,
                            batch=bl, n_q=sl, n_k=past, q_row0=tc, k_col0=0, v_col0=0, windowed=True,
                            self_kv=(nqb, nqb + swa_kvh), **swa_args)
            w_o = swa_w_o_b

        a = matmul2(a_c, a_l, w_o, BF16, tm=tm_mm, tn=tn_o, layer=j)
        x, h = resid_norm(x, a, norm_g4, mods, i, 1, 2, i, 2, 3, rows, tm_row)
        x = Stream(x)

        gated = ffn_in(h, ffn_w_in, ffn_conv_w, ffn_conv_b, i, rows, tm=rows.lat_seq, tn=tn_ffn, ch=ffn_ch)
        y = mm(gated, ffn_w_out, BF16, tn=tn_ffn, layer=i, single_x=True)
        if i + 1 < depth:
            x, h = resid_norm(x, y, norm_g4, mods, i, 3, 5, i + 1, 0, 0, rows, tm_row)
            x = Stream(x)
        else:
            y_ctx, y_lat = resid_split(x.parts[0], y, norm_g4, mods, i, 3, 5, rows, tm_row)

    stack = lambda xs: jnp.stack(xs, axis=1)
    return (y_ctx.reshape(bc, sc, d), y_lat.reshape(bl, sl, d),
            stack(outs["ckv"]), stack(outs["kpe"]), stack(outs["gk"]), stack(outs["gv"]),
            stack(outs["sk"]), stack(outs["sv"]))


def _row_tile(m):
    for tm in (1024, 512, 256, 128, 64, 32, 16, 8):
        if m % tm == 0:
            return tm
    return m
```
